```python
import math
import jax, jax.numpy as jnp
from jax import lax
import numpy as np

D_MODEL = 1024
BATCH = 16
SEQ = 2048
DEPTH = 1
DEC_BATCH = 128
DEC_SEQ = 1
PAST_LEN = 8192
PAGE_SIZE = 128

MIX_WIDTH = D_MODEL
ATTN_WIDTH = MIX_WIDTH // 2
RET_WIDTH = MIX_WIDTH - ATTN_WIDTH
ATTN_HEAD_DIM = 64
ATTN_HEADS = ATTN_WIDTH // ATTN_HEAD_DIM
RET_HEADS = 4
RET_HEAD_DIM = RET_WIDTH // RET_HEADS
DILATED_GROUPS = ((128, 1), (512, 4), (2048, 16))
WINDOW_MAX = max(w for w, _ in DILATED_GROUPS)
RET_CHUNK = 128
PEER_HEADS = 8
PEER_N_KEYS = 128
PEER_N_EXPERTS = PEER_N_KEYS ** 2
PEER_TOPK = 16
PEER_QUERY_DIM = 256
PEER_HALF = PEER_QUERY_DIM // 2
PEER_TOKEN_BLOCK = 128
NORM_EPS = 1e-6
GN_EPS = 1e-5
IN_COLS = 3 * ATTN_WIDTH + 4 * RET_WIDTH
SPLIT_POINTS = [ATTN_WIDTH, 2 * ATTN_WIDTH, 3 * ATTN_WIDTH,
                3 * ATTN_WIDTH + RET_WIDTH, 3 * ATTN_WIDTH + 2 * RET_WIDTH, 3 * ATTN_WIDTH + 3 * RET_WIDTH]

kernel_name = 'hymba_dilated_retention_peer_step'


def _alibi_slopes():
    n = ATTN_HEADS
    return jnp.asarray((2.0 ** (-8.0 * (np.arange(n) + 1) / n)).astype(np.float32))


def _ret_log_gamma():
    return jnp.asarray(np.log(1.0 - 2.0 ** (-5.0 - np.arange(RET_HEADS))).astype(np.float32))


def _rmsnorm(x, g):
    xf = x.astype(jnp.float32)
    r = lax.rsqrt(jnp.mean(xf * xf, -1, keepdims=True) + NORM_EPS)
    return (xf * r * g.astype(jnp.float32)).astype(x.dtype)


def _head_groupnorm(o, g):
    of = o.astype(jnp.float32)
    mu = jnp.mean(of, -1, keepdims=True)
    var = jnp.mean(jnp.square(of - mu), -1, keepdims=True)
    y = (of - mu) * lax.rsqrt(var + GN_EPS) * g.astype(jnp.float32).reshape(RET_HEADS, RET_HEAD_DIM)
    return y.astype(o.dtype)


def _softmax_stats(logits, valid):
    logits = jnp.where(valid, logits, -jnp.inf)
    m = jnp.max(logits, -1, keepdims=True)
    p = jnp.exp(logits - m)
    s = jnp.sum(p, -1, keepdims=True)
    return p / s, (m + jnp.log(s))[..., 0]


def _dilated_group_prompt(q, k, v, dil, n_steps, slopes):
    B, S, H, C = q.shape
    L = S // dil
    qb = math.gcd(L, 128)
    nblk = L // qb
    band = qb + n_steps
    qr = q.reshape(B, nblk, qb, dil, H, C)
    pad = ((0, 0), (n_steps, 0), (0, 0), (0, 0), (0, 0))
    kp = jnp.pad(k.reshape(B, L, dil, H, C), pad)
    vp = jnp.pad(v.reshape(B, L, dil, H, C), pad)
    idx = np.arange(nblk)[:, None] * qb + np.arange(band)[None, :]
    kb = kp[:, idx]
    vb = vp[:, idx]
    scores = jnp.einsum('bnqrhc,bnkrhc->bnrhqk', qr, kb).astype(jnp.float32) * (ATTN_HEAD_DIM ** -0.5)
    step = np.arange(qb)[:, None] + n_steps - np.arange(band)[None, :]
    true_key = idx[:, None, :] - n_steps
    valid = (step >= 0) & (step <= n_steps) & (true_key >= 0)
    dist = jnp.asarray((step * dil).astype(np.float32))
    logits = scores - slopes[:, None, None] * dist[None]
    p, lse = _softmax_stats(logits, valid[None, :, None, None])
    o = jnp.einsum('bnrhqk,bnkrhc->bnqrhc', p.astype(vb.dtype), vb).reshape(B, S, H, C)
    lse = jnp.transpose(lse, (0, 1, 4, 2, 3)).reshape(B, S, H)
    return o, lse


def _dilated_group_sample(q, kc, vc, dil, n_steps, slopes, offset):
    T = q.shape[1]
    j = np.arange(n_steps + 1)
    rows = offset + np.arange(T)[:, None] - j[None, :] * dil
    valid = rows >= 0
    rows_c = np.maximum(rows, 0)
    kg = kc[:, rows_c]
    vg = vc[:, rows_c]
    scores = jnp.einsum('bthc,btjhc->bthj', q, kg).astype(jnp.float32) * (ATTN_HEAD_DIM ** -0.5)
    dist = jnp.asarray((j * dil).astype(np.float32))
    logits = scores - slopes[:, None] * dist[None, :]
    p, lse = _softmax_stats(logits, valid[None, :, None, :])
    o = jnp.einsum('bthj,btjhc->bthc', p.astype(vg.dtype), vg)
    return o, lse


def _combine_groups(outs, lses):
    w = jax.nn.softmax(jnp.stack(lses, -1), -1)
    o = jnp.stack(outs, -2)
    return jnp.einsum('bthg,bthgc->bthc', w.astype(o.dtype), o)


def _retention_chunk(state, q, k, v, log_gamma):
    C = q.shape[1]
    dt = q.dtype
    pos = jnp.arange(C, dtype=jnp.float32)
    rel = pos[:, None] - pos[None, :]
    dmat = jnp.where(rel[None] >= 0, jnp.exp(jnp.maximum(rel, 0.0)[None] * log_gamma[:, None, None]), 0.0)
    inner = jnp.einsum('bqhd,bkhd->bhqk', q, k) * dmat.astype(dt)[None]
    o_in = jnp.einsum('bhqk,bkhv->bqhv', inner, v)
    q_dec = jnp.exp((pos[:, None] + 1.0) * log_gamma[None, :]).astype(dt)
    o_x = jnp.einsum('bqhd,bhdv->bqhv', q, state) * q_dec[None, :, :, None]
    k_dec = jnp.exp((C - 1.0 - pos)[:, None] * log_gamma[None, :]).astype(dt)
    s_new = (jnp.exp(C * log_gamma).astype(dt)[None, :, None, None] * state
             + jnp.einsum('bkhd,bkhv->bhdv', k * k_dec[None, :, :, None], v))
    return s_new, o_in + o_x


def _retention_prompt(q, k, v, log_gamma):
    B, S, H, dk = q.shape
    dv = v.shape[-1]
    n_chunks = S // RET_CHUNK

    def to_chunks(a):
        return jnp.moveaxis(a.reshape(B, n_chunks, RET_CHUNK, H, a.shape[-1]), 1, 0)

    def step(s, xs):
        qc, kc, vc = xs
        return _retention_chunk(s, qc, kc, vc, log_gamma)

    s0 = jnp.zeros((B, H, dk, dv), q.dtype)
    s_final, o = lax.scan(step, s0, (to_chunks(q), to_chunks(k), to_chunks(v)))
    return jnp.moveaxis(o, 0, 1).reshape(B, S, H, dv), s_final


def _mixer_inputs(x, norm_g, w_in):
    B, T, _ = x.shape
    xn = _rmsnorm(x, norm_g)
    aq, ak, av, rq, rk, rv, rg = jnp.split(xn @ w_in, SPLIT_POINTS, axis=-1)
    ah = lambda a: a.reshape(B, T, ATTN_HEADS, ATTN_HEAD_DIM)
    rh = lambda a: a.reshape(B, T, RET_HEADS, RET_HEAD_DIM)
    return ah(aq), ah(ak), ah(av), rh(rq), rh(rk) * (RET_HEAD_DIM ** -0.5), rh(rv), rg


def _mixer_output(attn_o, ret_o, gate, gn_g, w_out):
    B, T = attn_o.shape[:2]
    ret_y = jax.nn.silu(gate) * _head_groupnorm(ret_o, gn_g).reshape(B, T, RET_WIDTH)
    cat = jnp.concatenate([attn_o.reshape(B, T, ATTN_WIDTH), ret_y], -1)
    return cat @ w_out


def _peer_block(xb, w_pq, sub_keys, peer_u, peer_v):
    blk = xb.shape[0]
    q = (xb @ w_pq).reshape(blk, PEER_HEADS, 2, PEER_HALF)
    s = jnp.einsum('thcd,hcnd->thcn', q, sub_keys).astype(jnp.float32)
    sv, si = lax.top_k(s, PEER_TOPK)
    cand = (sv[:, :, 0, :, None] + sv[:, :, 1, None, :]).reshape(blk, PEER_HEADS, PEER_TOPK * PEER_TOPK)
    cidx = (si[:, :, 0, :, None] * PEER_N_KEYS + si[:, :, 1, None, :]).reshape(blk, PEER_HEADS, PEER_TOPK * PEER_TOPK)
    best, pos = lax.top_k(cand, PEER_TOPK)
    eid = jnp.take_along_axis(cidx, pos, -1)
    g = jax.nn.softmax(best, -1).astype(xb.dtype)
    u = peer_u[eid]
    a = jax.nn.gelu(jnp.einsum('thkd,td->thk', u, xb))
    return jnp.einsum('thk,thkd->td', g * a, peer_v[eid])


def _peer(xf, w_pq, sub_keys, peer_u, peer_v):
    n, d = xf.shape
    nb = -(-n // PEER_TOKEN_BLOCK)
    xp = jnp.pad(xf, ((0, nb * PEER_TOKEN_BLOCK - n), (0, 0))).reshape(nb, PEER_TOKEN_BLOCK, d)
    y = lax.map(lambda xb: _peer_block(xb, w_pq, sub_keys, peer_u, peer_v), xp)
    return y.reshape(nb * PEER_TOKEN_BLOCK, d)[:n]


def _channel(h, norm_g, w_pq, sub_keys, peer_u, peer_v):
    B, T, D = h.shape
    hn = _rmsnorm(h, norm_g).reshape(B * T, D)
    return h + _peer(hn, w_pq, sub_keys, peer_u, peer_v).reshape(B, T, D)


def setup_inputs(seed: int = 0) -> dict:
    key = jax.random.key(seed)
    ks = jax.random.split(key, 16)
    wbuf = min(WINDOW_MAX, PAST_LEN)
    nrm = lambda k, shape, s: jax.random.normal(k, shape, jnp.float32) * s
    return {
        'x_prompt': nrm(ks[0], (BATCH, SEQ, D_MODEL), 1.0),
        'x_sample': nrm(ks[1], (DEC_BATCH, DEC_SEQ, D_MODEL), 1.0),
        'cache_k_win': nrm(ks[2], (DEPTH, DEC_BATCH, wbuf, ATTN_HEADS, ATTN_HEAD_DIM), 1.0),
        'cache_v_win': nrm(ks[3], (DEPTH, DEC_BATCH, wbuf, ATTN_HEADS, ATTN_HEAD_DIM), 1.0),
        'state_ret': nrm(ks[4], (DEPTH, DEC_BATCH, RET_HEADS, RET_HEAD_DIM, RET_HEAD_DIM), 0.5),
        'norm1_g': 1.0 + nrm(ks[5], (DEPTH, D_MODEL), 0.01),
        'w_in': nrm(ks[6], (DEPTH, D_MODEL, IN_COLS), D_MODEL ** -0.5),
        'ret_gn_g': 1.0 + nrm(ks[7], (DEPTH, RET_WIDTH), 0.01),
        'w_out': nrm(ks[8], (DEPTH, MIX_WIDTH, D_MODEL), MIX_WIDTH ** -0.5),
        'norm2_g': 1.0 + nrm(ks[9], (DEPTH, D_MODEL), 0.01),
        'w_pq': nrm(ks[10], (DEPTH, D_MODEL, PEER_HEADS * PEER_QUERY_DIM), D_MODEL ** -0.5),
        'peer_sub_keys': nrm(ks[11], (DEPTH, PEER_HEADS, 2, PEER_N_KEYS, PEER_HALF), PEER_HALF ** -0.5),
        'peer_u': nrm(ks[12], (DEPTH, PEER_N_EXPERTS, D_MODEL), D_MODEL ** -0.5),
        'peer_v': nrm(ks[13], (DEPTH, PEER_N_EXPERTS, D_MODEL), (PEER_HEADS * PEER_TOPK) ** -0.5),
        'norm_f_g': 1.0 + nrm(ks[14], (D_MODEL,), 0.01),
    }


def reference(x_prompt, x_sample, cache_k_win, cache_v_win, state_ret, norm1_g, w_in, ret_gn_g, w_out,
              norm2_g, w_pq, peer_sub_keys, peer_u, peer_v, norm_f_g):
    slopes = _alibi_slopes()
    log_gamma = _ret_log_gamma()
    S = x_prompt.shape[1]
    keep = min(WINDOW_MAX, S)
    offset = cache_k_win.shape[2]
    hp, hs = x_prompt, x_sample
    kp_l, vp_l, rp_l, ks_l, vs_l, rs_l = [], [], [], [], [], []
    for l in range(DEPTH):
        aq, ak, av, rq, rk, rv, rg = _mixer_inputs(hp, norm1_g[l], w_in[l])
        outs, lses = [], []
        for window, dil in DILATED_GROUPS:
            o, lse = _dilated_group_prompt(aq, ak, av, dil, window // dil, slopes)
            outs.append(o)
            lses.append(lse)
        attn_o = _combine_groups(outs, lses)
        ret_o, s_fin = _retention_prompt(rq, rk, rv, log_gamma)
        h = hp + _mixer_output(attn_o, ret_o, rg, ret_gn_g[l], w_out[l])
        hp = _channel(h, norm2_g[l], w_pq[l], peer_sub_keys[l], peer_u[l], peer_v[l])
        kp_l.append(ak[:, S - keep:])
        vp_l.append(av[:, S - keep:])
        rp_l.append(s_fin)
        aq, ak, av, rq, rk, rv, rg = _mixer_inputs(hs, norm1_g[l], w_in[l])
        kc = jnp.concatenate([cache_k_win[l], ak], 1)
        vc = jnp.concatenate([cache_v_win[l], av], 1)
        outs, lses = [], []
        for window, dil in DILATED_GROUPS:
            o, lse = _dilated_group_sample(aq, kc, vc, dil, window // dil, slopes, offset)
            outs.append(o)
            lses.append(lse)
        attn_o = _combine_groups(outs, lses)
        s_new, ret_o = _retention_chunk(state_ret[l], rq, rk, rv, log_gamma)
        h = hs + _mixer_output(attn_o, ret_o, rg, ret_gn_g[l], w_out[l])
        hs = _channel(h, norm2_g[l], w_pq[l], peer_sub_keys[l], peer_u[l], peer_v[l])
        ks_l.append(ak)
        vs_l.append(av)
        rs_l.append(s_new)
    y_prompt = _rmsnorm(hp, norm_f_g)
    y_sample = _rmsnorm(hs, norm_f_g)
    return (y_prompt, y_sample, jnp.stack(kp_l, 0), jnp.stack(vp_l, 0), jnp.stack(rp_l, 0),
            jnp.stack(ks_l, 0), jnp.stack(vs_l, 0), jnp.stack(rs_l, 0))
```

```python
import functools

import numpy as np
import jax
import jax.numpy as jnp
from jax import lax
from jax.experimental import pallas as pl
from jax.experimental.pallas import tpu as pltpu

F32 = jnp.float32
MXU_DTYPE = jnp.bfloat16

D_MODEL = 1024
ATTN_WIDTH = 512
RET_WIDTH = 512
ATTN_HEAD_DIM = 64
ATTN_HEADS = 8
RET_HEADS = 4
RET_HEAD_DIM = 128
DILATIONS = (1, 4, 16)
N_STEPS = 128
ATTN_BLOCK = 128
RET_CHUNK = 128
PEER_HEADS = 8
PEER_N_KEYS = 128
PEER_TOPK = 16
PEER_HALF = 128
PEER_SLOTS = PEER_HEADS * PEER_TOPK
NORM_EPS = 1e-6
GN_EPS = 1e-5
IN_SPLITS = 7
RET_K_SCALE = RET_HEAD_DIM ** -0.5
ATTN_SCALE = ATTN_HEAD_DIM ** -0.5
LANES = 128
VMEM_LIMIT = 48 * 1024 * 1024

PEER_TOKENS_PER_STEP = 128
PEER_CHUNK = 8


def _params(*sem):
    return pltpu.CompilerParams(dimension_semantics=sem, vmem_limit_bytes=VMEM_LIMIT)


def _rms(x, g):
    r = lax.rsqrt(jnp.mean(x * x, axis=-1, keepdims=True) + NORM_EPS)
    return x * r * g


def _inproj_kernel(x_ref, g_ref, w_ref, *out_refs):
    xn = _rms(x_ref[...], g_ref[...]).astype(MXU_DTYPE)
    for i, o_ref in enumerate(out_refs):
        y = jnp.dot(xn, w_ref[:, i * ATTN_WIDTH:(i + 1) * ATTN_WIDTH], preferred_element_type=F32)
        if i == 4:
            y = y * RET_K_SCALE
        o_ref[...] = y


def _inproj(x, g, w):
    n = x.shape[0]
    tm = min(256, n)
    out = jax.ShapeDtypeStruct((n, ATTN_WIDTH), F32)
    return pl.pallas_call(
        _inproj_kernel,
        grid=(n // tm,),
        in_specs=[pl.BlockSpec((tm, D_MODEL), lambda i: (i, 0)),
                  pl.BlockSpec((1, D_MODEL), lambda i: (0, 0)),
                  pl.BlockSpec((D_MODEL, IN_SPLITS * ATTN_WIDTH), lambda i: (0, 0))],
        out_specs=[pl.BlockSpec((tm, ATTN_WIDTH), lambda i: (i, 0))] * IN_SPLITS,
        out_shape=[out] * IN_SPLITS,
        compiler_params=_params("parallel"),
    )(x, g.reshape(1, D_MODEL), w)


def _band_attn_kernel(dil, nblk, q_ref, k_ref, v_ref, slope_ref, o_ref, lse_ref):
    lane = lax.broadcasted_iota(jnp.int32, (ATTN_BLOCK, LANES), 1)
    qi = lax.broadcasted_iota(jnp.int32, (ATTN_BLOCK, ATTN_BLOCK), 0)
    kj = lax.broadcasted_iota(jnp.int32, (ATTN_BLOCK, ATTN_BLOCK), 1)
    dist_cur = ((qi - kj) * dil).astype(F32)
    dist_prev = ((qi - kj + ATTN_BLOCK) * dil).astype(F32)
    ok_cur = kj <= qi
    ok_prev = kj >= qi
    nt = (((1,), (1,)), ((), ()))

    def block(n, carry):
        row = pl.multiple_of(n * ATTN_BLOCK, ATTN_BLOCK)
        prow = pl.multiple_of(jnp.maximum(n - 1, 0) * ATTN_BLOCK, ATTN_BLOCK)
        q = q_ref[pl.ds(row, ATTN_BLOCK), :]
        kc = k_ref[pl.ds(row, ATTN_BLOCK), :].astype(MXU_DTYPE)
        vc = v_ref[pl.ds(row, ATTN_BLOCK), :].astype(MXU_DTYPE)
        kp = k_ref[pl.ds(prow, ATTN_BLOCK), :].astype(MXU_DTYPE)
        vp = v_ref[pl.ds(prow, ATTN_BLOCK), :].astype(MXU_DTYPE)
        okp = ok_prev & (kj + row >= ATTN_BLOCK)
        outs, lses = [], []
        for hh in range(2):
            in_head = (lane >= hh * ATTN_HEAD_DIM) & (lane < (hh + 1) * ATTN_HEAD_DIM)
            qm = jnp.where(in_head, q, 0.0).astype(MXU_DTYPE)
            slope = slope_ref[hh, 0:1, :]
            sc = lax.dot_general(qm, kc, nt, preferred_element_type=F32) * ATTN_SCALE - slope * dist_cur
            sp = lax.dot_general(qm, kp, nt, preferred_element_type=F32) * ATTN_SCALE - slope * dist_prev
            sc = jnp.where(ok_cur, sc, -jnp.inf)
            sp = jnp.where(okp, sp, -jnp.inf)
            m = jnp.maximum(jnp.max(sc, -1, keepdims=True), jnp.max(sp, -1, keepdims=True))
            pc = jnp.exp(sc - m)
            pp = jnp.exp(sp - m)
            s = jnp.sum(pc, -1, keepdims=True) + jnp.sum(pp, -1, keepdims=True)
            pc = (pc / s).astype(MXU_DTYPE)
            pp = (pp / s).astype(MXU_DTYPE)
            o = (jnp.dot(pc, vc, preferred_element_type=F32) + jnp.dot(pp, vp, preferred_element_type=F32))
            outs.append(o)
            lses.append(jnp.broadcast_to(m + jnp.log(s), (ATTN_BLOCK, LANES)))
        first = lane < ATTN_HEAD_DIM
        o_ref[pl.ds(row, ATTN_BLOCK), :] = jnp.where(first, outs[0], outs[1])
        lse_ref[pl.ds(row, ATTN_BLOCK), :] = jnp.where(first, lses[0], lses[1])
        return carry

    lax.fori_loop(0, nblk, block, 0)


def _band_attention(q, k, v, dil, slopes_tile):
    b, s, _ = q.shape
    l = s // dil
    nblk = l // ATTN_BLOCK
    if dil > 1:
        to_res = lambda a: a.reshape(b, l, dil, ATTN_WIDTH).transpose(0, 2, 1, 3)
        q, k, v = to_res(q), to_res(k), to_res(v)
    else:
        q, k, v = (a.reshape(b, 1, l, ATTN_WIDTH) for a in (q, k, v))
    spec = pl.BlockSpec((None, None, l, LANES), lambda i, r, h: (i, r, 0, h))
    out = jax.ShapeDtypeStruct((b, dil, l, ATTN_WIDTH), F32)
    o, lse = pl.pallas_call(
        functools.partial(_band_attn_kernel, dil, nblk),
        grid=(b, dil, ATTN_WIDTH // LANES),
        in_specs=[spec, spec, spec,
                  pl.BlockSpec((None, 2, 8, LANES), lambda i, r, h: (h, 0, 0, 0))],
        out_specs=[spec, spec],
        out_shape=[out, out],
        compiler_params=_params("parallel", "parallel", "parallel"),
    )(q, k, v, slopes_tile)
    if dil > 1:
        from_res = lambda a: a.transpose(0, 2, 1, 3).reshape(b, s, ATTN_WIDTH)
        return from_res(o), from_res(lse)
    return o.reshape(b, s, ATTN_WIDTH), lse.reshape(b, s, ATTN_WIDTH)


def _sample_attn_kernel(q_ref, kn_ref, vn_ref, slope_ref, *refs):
    kv_refs, out_refs = refs[:6], refs[6:]
    head = lax.broadcasted_iota(jnp.int32, (ATTN_HEADS, ATTN_WIDTH), 0)
    lane = lax.broadcasted_iota(jnp.int32, (ATTN_HEADS, ATTN_WIDTH), 1)
    head_mask = (lane // ATTN_HEAD_DIM == head).astype(F32)
    q_exp = q_ref[...] * head_mask
    slope = slope_ref[...]
    steps_back = (N_STEPS - lax.broadcasted_iota(jnp.int32, (ATTN_HEADS, N_STEPS), 1)).astype(F32)
    l0 = jnp.sum(q_exp * kn_ref[...], -1, keepdims=True) * ATTN_SCALE
    nt = (((1,), (1,)), ((), ()))
    for g, dil in enumerate(DILATIONS):
        kc = kv_refs[2 * g][...].astype(MXU_DTYPE)
        vc = kv_refs[2 * g + 1][...].astype(MXU_DTYPE)
        sc = lax.dot_general(q_exp.astype(MXU_DTYPE), kc, nt, preferred_element_type=F32)
        sc = sc * ATTN_SCALE - slope * (steps_back * float(dil))
        m = jnp.maximum(jnp.max(sc, -1, keepdims=True), l0)
        p = jnp.exp(sc - m)
        p0 = jnp.exp(l0 - m)
        s = jnp.sum(p, -1, keepdims=True) + p0
        o = jnp.dot((p / s).astype(MXU_DTYPE), vc, preferred_element_type=F32) + (p0 / s) * vn_ref[...]
        out_refs[2 * g][...] = jnp.sum(o * head_mask, 0, keepdims=True)
        out_refs[2 * g + 1][...] = jnp.sum((m + jnp.log(s)) * head_mask, 0, keepdims=True)


def _sample_attention(q, kn, vn, cache_k, cache_v, slopes_rows):
    b, w = cache_k.shape[0], cache_k.shape[1]
    row = lambda a: a.reshape(b, 1, ATTN_WIDTH)
    row_spec = pl.BlockSpec((None, 1, ATTN_WIDTH), lambda i: (i, 0, 0))
    kv, kv_specs = [], []
    for dil in DILATIONS:
        last = w // dil // N_STEPS - 1
        for c in (cache_k, cache_v):
            kv.append(c.reshape(b, w // dil, dil * ATTN_WIDTH))
            kv_specs.append(pl.BlockSpec((None, N_STEPS, ATTN_WIDTH), lambda i, last=last: (i, last, 0)))
    out = jax.ShapeDtypeStruct((b, 1, ATTN_WIDTH), F32)
    res = pl.pallas_call(
        _sample_attn_kernel,
        grid=(b,),
        in_specs=[row_spec, row_spec, row_spec, pl.BlockSpec((ATTN_HEADS, N_STEPS), lambda i: (0, 0))] + kv_specs,
        out_specs=[row_spec] * 6,
        out_shape=[out] * 6,
        compiler_params=_params("parallel"),
    )(row(q), row(kn), row(vn), slopes_rows, *kv)
    return [a.reshape(b, ATTN_WIDTH) for a in res]


def _ret_prompt_kernel(nchunk, q_ref, k_ref, v_ref, dmat_ref, qdec_ref, kdec_ref, cdec_ref, o_ref, s_ref):
    s_ref[...] = jnp.zeros_like(s_ref)
    nt = (((1,), (1,)), ((), ()))
    tn = (((0,), (0,)), ((), ()))

    def chunk(c, carry):
        row = pl.multiple_of(c * RET_CHUNK, RET_CHUNK)
        q = q_ref[pl.ds(row, RET_CHUNK), :]
        k = k_ref[pl.ds(row, RET_CHUNK), :]
        v = v_ref[pl.ds(row, RET_CHUNK), :].astype(MXU_DTYPE)
        qb = q.astype(MXU_DTYPE)
        state = s_ref[...]
        inner = lax.dot_general(qb, k.astype(MXU_DTYPE), nt, preferred_element_type=F32) * dmat_ref[...]
        o_in = jnp.dot(inner.astype(MXU_DTYPE), v, preferred_element_type=F32)
        o_x = jnp.dot(qb, state.astype(MXU_DTYPE), preferred_element_type=F32) * qdec_ref[...]
        o_ref[pl.ds(row, RET_CHUNK), :] = o_in + o_x
        kd = (k * kdec_ref[...]).astype(MXU_DTYPE)
        s_ref[...] = cdec_ref[0:1, :] * state + lax.dot_general(kd, v, tn, preferred_element_type=F32)
        return carry

    lax.fori_loop(0, nchunk, chunk, 0)


def _ret_prompt(q, k, v, consts):
    b, s, _ = q.shape
    seq = pl.BlockSpec((None, s, RET_HEAD_DIM), lambda i, h: (i, 0, h))
    per_head = lambda rows: pl.BlockSpec((None, rows, RET_HEAD_DIM), lambda i, h: (h, 0, 0))
    return pl.pallas_call(
        functools.partial(_ret_prompt_kernel, s // RET_CHUNK),
        grid=(b, RET_HEADS),
        in_specs=[seq, seq, seq, per_head(RET_CHUNK), per_head(RET_CHUNK), per_head(RET_CHUNK), per_head(8)],
        out_specs=[seq, pl.BlockSpec((None, None, RET_HEAD_DIM, RET_HEAD_DIM), lambda i, h: (i, h, 0, 0))],
        out_shape=[jax.ShapeDtypeStruct((b, s, RET_WIDTH), F32),
                   jax.ShapeDtypeStruct((b, RET_HEADS, RET_HEAD_DIM, RET_HEAD_DIM), F32)],
        compiler_params=_params("parallel", "parallel"),
    )(q, k, v, *consts)


def _ret_sample_kernel(qc_ref, kc_ref, v_ref, s_ref, gam_ref, o_ref, sn_ref):
    for h in range(RET_HEADS):
        qc = qc_ref[h]
        kc = kc_ref[h]
        v = v_ref[:, h * RET_HEAD_DIM:(h + 1) * RET_HEAD_DIM]
        gam = gam_ref[h, 0:1, :]
        state = s_ref[h]
        inner = jnp.sum(qc * kc, 0, keepdims=True)
        o_x = jnp.sum(qc * state, 0, keepdims=True)
        o_ref[:, h * RET_HEAD_DIM:(h + 1) * RET_HEAD_DIM] = inner * v + gam * o_x
        sn_ref[h] = gam * state + kc * v


def _ret_sample(q, k, v, state, gam_tile):
    b = q.shape[0]
    col = lambda a: a.reshape(b, RET_HEADS, RET_HEAD_DIM, 1)
    col_spec = pl.BlockSpec((None, RET_HEADS, RET_HEAD_DIM, 1), lambda i: (i, 0, 0, 0))
    row_spec = pl.BlockSpec((None, 1, RET_WIDTH), lambda i: (i, 0, 0))
    st_spec = pl.BlockSpec((None, RET_HEADS, RET_HEAD_DIM, RET_HEAD_DIM), lambda i: (i, 0, 0, 0))
    o, sn = pl.pallas_call(
        _ret_sample_kernel,
        grid=(b,),
        in_specs=[col_spec, col_spec, row_spec, st_spec,
                  pl.BlockSpec((RET_HEADS, 8, RET_HEAD_DIM), lambda i: (0, 0, 0))],
        out_specs=[row_spec, st_spec],
        out_shape=[jax.ShapeDtypeStruct((b, 1, RET_WIDTH), F32), jax.ShapeDtypeStruct(state.shape, F32)],
        compiler_params=_params("parallel"),
    )(col(q), col(k), v.reshape(b, 1, RET_WIDTH), state, gam_tile)
    return o.reshape(b, RET_WIDTH), sn


def _mixer_out_kernel(o1_ref, o2_ref, o3_ref, l1_ref, l2_ref, l3_ref, ret_ref, gate_ref, x_ref,
                      gn_ref, wout_ref, n2_ref, wpq_ref, keys_ref, h_ref, hn_ref, st_ref):
    l1, l2, l3 = l1_ref[...], l2_ref[...], l3_ref[...]
    m = jnp.maximum(jnp.maximum(l1, l2), l3)
    e1, e2, e3 = jnp.exp(l1 - m), jnp.exp(l2 - m), jnp.exp(l3 - m)
    den = e1 + e2 + e3
    attn = (e1 / den) * o1_ref[...] + (e2 / den) * o2_ref[...] + (e3 / den) * o3_ref[...]
    gate = gate_ref[...]
    swish = gate * (1.0 / (1.0 + jnp.exp(-gate)))
    normed = []
    for hd in range(RET_HEADS):
        cols = slice(hd * RET_HEAD_DIM, (hd + 1) * RET_HEAD_DIM)
        r = ret_ref[:, cols]
        mu = jnp.mean(r, -1, keepdims=True)
        var = jnp.mean(jnp.square(r - mu), -1, keepdims=True)
        normed.append((r - mu) * lax.rsqrt(var + GN_EPS) * gn_ref[:, cols])
    ret_y = swish * jnp.concatenate(normed, -1)
    h = (x_ref[...]
         + jnp.dot(attn.astype(MXU_DTYPE), wout_ref[0:ATTN_WIDTH, :], preferred_element_type=F32)
         + jnp.dot(ret_y.astype(MXU_DTYPE), wout_ref[ATTN_WIDTH:, :], preferred_element_type=F32))
    h_ref[...] = h
    hn = _rms(h, n2_ref[...])
    hn_ref[...] = hn
    hb = hn.astype(MXU_DTYPE)
    nt = (((1,), (1,)), ((), ()))
    for j in range(2 * PEER_HEADS):
        qj = jnp.dot(hb, wpq_ref[:, j * PEER_HALF:(j + 1) * PEER_HALF], preferred_element_type=F32)
        st_ref[j] = lax.dot_general(keys_ref[j], qj.astype(MXU_DTYPE), nt, preferred_element_type=F32)


def _mixer_out(o1, o2, o3, l1, l2, l3, ret_o, gate, x, gn_g, w_out, n2_g, w_pq, keys):
    n = x.shape[0]
    tm = min(256, n)
    half = pl.BlockSpec((tm, ATTN_WIDTH), lambda i: (i, 0))
    full = pl.BlockSpec((tm, D_MODEL), lambda i: (i, 0))
    const = lambda shape: pl.BlockSpec(shape, lambda i: (0,) * len(shape))
    return pl.pallas_call(
        _mixer_out_kernel,
        grid=(n // tm,),
        in_specs=[half] * 8 + [full, const((1, RET_WIDTH)), const((D_MODEL, D_MODEL)), const((1, D_MODEL)),
                               const((D_MODEL, 2 * PEER_HEADS * PEER_HALF)),
                               const((2 * PEER_HEADS, PEER_N_KEYS, PEER_HALF))],
        out_specs=[full, full, pl.BlockSpec((2 * PEER_HEADS, PEER_N_KEYS, tm), lambda i: (0, 0, i))],
        out_shape=[jax.ShapeDtypeStruct((n, D_MODEL), F32), jax.ShapeDtypeStruct((n, D_MODEL), F32),
                   jax.ShapeDtypeStruct((2 * PEER_HEADS, PEER_N_KEYS, n), F32)],
        compiler_params=_params("parallel"),
    )(o1, o2, o3, l1, l2, l3, ret_o, gate, x, gn_g.reshape(1, RET_WIDTH), w_out, n2_g.reshape(1, D_MODEL), w_pq, keys)


def _top_rows(s, k, payload=None):
    nrow = s.shape[0]
    ridx = lax.broadcasted_iota(jnp.int32, s.shape, 0)
    vals, picks = [], []
    for _ in range(k):
        m = jnp.max(s, 0, keepdims=True)
        am = jnp.min(jnp.where(s == m, ridx, nrow), 0, keepdims=True)
        hit = ridx == am
        vals.append(m)
        picks.append(am if payload is None else jnp.sum(jnp.where(hit, payload, 0), 0, keepdims=True))
        s = jnp.where(hit, -jnp.inf, s)
    return jnp.concatenate(vals, 0), jnp.concatenate(picks, 0)


def _topk_kernel(st_ref, eid_ref, g_ref):
    sv0, si0 = _top_rows(st_ref[0], PEER_TOPK)
    sv1, si1 = _top_rows(st_ref[1], PEER_TOPK)
    cand = jnp.concatenate([sv0[a:a + 1] + sv1 for a in range(PEER_TOPK)], 0)
    cidx = jnp.concatenate([si0[a:a + 1] * PEER_N_KEYS + si1 for a in range(PEER_TOPK)], 0)
    best, eid = _top_rows(cand, PEER_TOPK, payload=cidx)
    e = jnp.exp(best - jnp.max(best, 0, keepdims=True))
    eid_ref[...] = eid
    g_ref[...] = e / jnp.sum(e, 0, keepdims=True)


def _topk(st):
    n = st.shape[-1]
    tn = LANES
    out_spec = pl.BlockSpec((PEER_TOPK, tn), lambda i, h: (h, i))
    return pl.pallas_call(
        _topk_kernel,
        grid=(n // tn, PEER_HEADS),
        in_specs=[pl.BlockSpec((2, PEER_N_KEYS, tn), lambda i, h: (h, 0, i))],
        out_specs=[out_spec, out_spec],
        out_shape=[jax.ShapeDtypeStruct((PEER_SLOTS, n), jnp.int32), jax.ShapeDtypeStruct((PEER_SLOTS, n), F32)],
        compiler_params=_params("parallel", "parallel"),
    )(st)


def _peer_kernel(eid_hbm, gt_ref, hn_ref, h_ref, nf_ref, u_hbm, v_hbm, y_ref, eid_smem, ubuf, vbuf, sems):
    step = pl.program_id(0)
    nchunk = PEER_TOKENS_PER_STEP // PEER_CHUNK
    rows = PEER_CHUNK * PEER_SLOTS
    ids = pltpu.make_async_copy(eid_hbm.at[step], eid_smem, sems.at[2, 0])
    ids.start()
    ids.wait()

    def gather(c, slot):
        def one(i, carry):
            e = eid_smem[c * rows + i]
            pltpu.make_async_copy(u_hbm.at[pl.ds(e, 1)], ubuf.at[slot, pl.ds(i, 1)], sems.at[0, slot]).start()
            pltpu.make_async_copy(v_hbm.at[pl.ds(e, 1)], vbuf.at[slot, pl.ds(i, 1)], sems.at[1, slot]).start()
            return carry
        lax.fori_loop(0, rows, one, 0, unroll=8)

    def wait(slot):
        pltpu.make_async_copy(u_hbm.at[pl.ds(0, rows)], ubuf.at[slot], sems.at[0, slot]).wait()
        pltpu.make_async_copy(v_hbm.at[pl.ds(0, rows)], vbuf.at[slot], sems.at[1, slot]).wait()

    tok_lane = lax.broadcasted_iota(jnp.int32, (PEER_SLOTS, PEER_TOKENS_PER_STEP), 1)
    gather(0, 0)

    def chunk(c, carry):
        slot = c % 2

        @pl.when(c + 1 < nchunk)
        def _():
            gather(c + 1, 1 - slot)

        wait(slot)
        for t in range(PEER_CHUNK):
            tok = c * PEER_CHUNK + t
            x = hn_ref[pl.ds(tok, 1), :]
            u = ubuf[slot, t * PEER_SLOTS:(t + 1) * PEER_SLOTS, :]
            a = jax.nn.gelu(jnp.sum(u * x, -1, keepdims=True))
            gate = jnp.sum(jnp.where(tok_lane == tok, gt_ref[...], 0.0), -1, keepdims=True)
            v = vbuf[slot, t * PEER_SLOTS:(t + 1) * PEER_SLOTS, :]
            out = jnp.sum((gate * a) * v, 0, keepdims=True)
            y_ref[pl.ds(tok, 1), :] = _rms(h_ref[pl.ds(tok, 1), :] + out, nf_ref[...])
        return carry

    lax.fori_loop(0, nchunk, chunk, 0)


def _peer(eid_t, g_t, hn, h, nf_g, peer_u, peer_v):
    n = hn.shape[0]
    tp = PEER_TOKENS_PER_STEP
    eid_tok = eid_t.T.reshape(n // tp, tp * PEER_SLOTS)
    rows = PEER_CHUNK * PEER_SLOTS
    tok = pl.BlockSpec((tp, D_MODEL), lambda i: (i, 0))
    return pl.pallas_call(
        _peer_kernel,
        grid=(n // tp,),
        in_specs=[pl.BlockSpec(memory_space=pl.ANY),
                  pl.BlockSpec((PEER_SLOTS, tp), lambda i: (0, i)),
                  tok, tok,
                  pl.BlockSpec((1, D_MODEL), lambda i: (0, 0)),
                  pl.BlockSpec(memory_space=pl.ANY),
                  pl.BlockSpec(memory_space=pl.ANY)],
        out_specs=tok,
        out_shape=jax.ShapeDtypeStruct((n, D_MODEL), F32),
        scratch_shapes=[pltpu.SMEM((tp * PEER_SLOTS,), jnp.int32),
                        pltpu.VMEM((2, rows, D_MODEL), F32),
                        pltpu.VMEM((2, rows, D_MODEL), F32),
                        pltpu.SemaphoreType.DMA((3, 2))],
        compiler_params=_params("arbitrary"),
    )(eid_tok, g_t, hn, h, nf_g.reshape(1, D_MODEL), peer_u, peer_v)


def _alibi_slopes():
    return (2.0 ** (-8.0 * (np.arange(ATTN_HEADS) + 1) / ATTN_HEADS)).astype(np.float32)


def _ret_consts():
    log_gamma = np.log(1.0 - 2.0 ** (-5.0 - np.arange(RET_HEADS))).astype(np.float32)
    pos = np.arange(RET_CHUNK, dtype=np.float32)
    rel = pos[:, None] - pos[None, :]
    dmat = np.where(rel[None] >= 0, np.exp(np.maximum(rel, 0.0)[None] * log_gamma[:, None, None]), 0.0)
    lanes = lambda col: np.broadcast_to(col[:, :, None], (RET_HEADS, col.shape[1], RET_HEAD_DIM))
    qdec = lanes(np.exp((pos[None, :] + 1.0) * log_gamma[:, None]))
    kdec = lanes(np.exp((RET_CHUNK - 1.0 - pos)[None, :] * log_gamma[:, None]))
    cdec = lanes(np.broadcast_to(np.exp(RET_CHUNK * log_gamma)[:, None], (RET_HEADS, 8)))
    gam = lanes(np.broadcast_to(np.exp(log_gamma)[:, None], (RET_HEADS, 8)))
    as_f32 = lambda a: jnp.asarray(np.ascontiguousarray(a, dtype=np.float32))
    return tuple(as_f32(a) for a in (dmat, qdec, kdec, cdec)), as_f32(gam)


def _channel_and_norm(o1, o2, o3, l1, l2, l3, ret_o, gate, x, gn_g, w_out, n2_g, w_pq, keys, peer_u, peer_v, nf_g):
    h, hn, st = _mixer_out(o1, o2, o3, l1, l2, l3, ret_o, gate, x, gn_g, w_out, n2_g, w_pq, keys)
    eid_t, g_t = _topk(st)
    return _peer(eid_t, g_t, hn, h, nf_g, peer_u, peer_v)


def kernel(x_prompt, x_sample, cache_k_win, cache_v_win, state_ret, norm1_g, w_in, ret_gn_g, w_out, norm2_g, w_pq, peer_sub_keys, peer_u, peer_v, norm_f_g):
    depth = w_in.shape[0]
    assert depth == 1, "single-layer stack"
    b, s, _ = x_prompt.shape
    bs = x_sample.shape[0]
    assert x_sample.shape[1] == 1 and s % (ATTN_BLOCK * max(DILATIONS)) == 0
    assert cache_k_win.shape[2] % (N_STEPS * max(DILATIONS)) == 0

    slopes = _alibi_slopes()
    slopes_tile = jnp.asarray(np.broadcast_to(slopes.reshape(ATTN_HEADS // 2, 2, 1, 1), (ATTN_HEADS // 2, 2, 8, LANES)).copy())
    slopes_rows = jnp.asarray(np.broadcast_to(slopes[:, None], (ATTN_HEADS, N_STEPS)).copy())
    ret_consts, gam_tile = _ret_consts()

    w_in_b = w_in[0].astype(MXU_DTYPE)
    w_out_b = w_out[0].astype(MXU_DTYPE)
    w_pq_b = w_pq[0].astype(MXU_DTYPE)
    keys_b = peer_sub_keys[0].reshape(2 * PEER_HEADS, PEER_N_KEYS, PEER_HALF).astype(MXU_DTYPE)
    tail = (ret_gn_g[0], w_out_b, norm2_g[0], w_pq_b, keys_b, peer_u[0], peer_v[0], norm_f_g)

    xp = x_prompt.reshape(b * s, D_MODEL)
    aq, ak, av, rq, rk, rv, rg = _inproj(xp, norm1_g[0], w_in_b)
    seq = lambda a: a.reshape(b, s, ATTN_WIDTH)
    outs, lses = [], []
    for dil in DILATIONS:
        o, lse = _band_attention(seq(aq), seq(ak), seq(av), dil, slopes_tile)
        outs.append(o.reshape(b * s, ATTN_WIDTH))
        lses.append(lse.reshape(b * s, ATTN_WIDTH))
    ret_o, s_fin = _ret_prompt(seq(rq), seq(rk), seq(rv), ret_consts)
    y_prompt = _channel_and_norm(*outs, *lses, ret_o.reshape(b * s, RET_WIDTH), rg, xp, *tail)
    keep = min(N_STEPS * max(DILATIONS), s)
    k_win = ak.reshape(b, s, ATTN_HEADS, ATTN_HEAD_DIM)[:, s - keep:]
    v_win = av.reshape(b, s, ATTN_HEADS, ATTN_HEAD_DIM)[:, s - keep:]

    xs = x_sample.reshape(bs, D_MODEL)
    aq, ak, av, rq, rk, rv, rg = _inproj(xs, norm1_g[0], w_in_b)
    o1, l1, o2, l2, o3, l3 = _sample_attention(aq, ak, av, cache_k_win[0], cache_v_win[0], slopes_rows)
    ret_o, s_new = _ret_sample(rq, rk, rv, state_ret[0], gam_tile)
    y_sample = _channel_and_norm(o1, o2, o3, l1, l2, l3, ret_o, rg, xs, *tail)

    return (y_prompt.reshape(b, s, D_MODEL), y_sample.reshape(bs, 1, D_MODEL),
            k_win[None], v_win[None], s_fin[None],
            ak.reshape(1, bs, 1, ATTN_HEADS, ATTN_HEAD_DIM), av.reshape(1, bs, 1, ATTN_HEADS, ATTN_HEAD_DIM),
            s_new[None])
```

```python
import functools

import numpy as np
import jax
import jax.numpy as jnp
from jax import lax
from jax.experimental import pallas as pl
from jax.experimental.pallas import tpu as pltpu

F32 = jnp.float32
MXU_DTYPE = jnp.bfloat16

D_MODEL = 1024
ATTN_WIDTH = 512
RET_WIDTH = 512
ATTN_HEAD_DIM = 64
ATTN_HEADS = 8
RET_HEADS = 4
RET_HEAD_DIM = 128
DILATIONS = (1, 4, 16)
N_STEPS = 128
ATTN_BLOCK = 128
ATTN_PAD = N_STEPS * max(DILATIONS)
RET_CHUNK = 128
PEER_HEADS = 8
PEER_N_KEYS = 128
PEER_TOPK = 16
PEER_HALF = 128
PEER_SLOTS = PEER_HEADS * PEER_TOPK
NORM_EPS = 1e-6
GN_EPS = 1e-5
IN_SPLITS = 7
RET_K_SCALE = RET_HEAD_DIM ** -0.5
ATTN_SCALE = ATTN_HEAD_DIM ** -0.5
LANES = 128
VMEM_LIMIT = 48 * 1024 * 1024

PEER_TOKENS_PER_STEP = 128
PEER_CHUNK = 8


def _params(*sem):
    return pltpu.CompilerParams(dimension_semantics=sem, vmem_limit_bytes=VMEM_LIMIT)


def _rms(x, g):
    r = lax.rsqrt(jnp.mean(x * x, axis=-1, keepdims=True) + NORM_EPS)
    return x * r * g


def _inproj_kernel(x_ref, g_ref, w_ref, *out_refs):
    xn = _rms(x_ref[...], g_ref[...]).astype(MXU_DTYPE)
    for i, o_ref in enumerate(out_refs):
        y = jnp.dot(xn, w_ref[:, i * ATTN_WIDTH:(i + 1) * ATTN_WIDTH], preferred_element_type=F32)
        if i == 4:
            y = y * RET_K_SCALE
        o_ref[...] = y


def _inproj(x, g, w):
    n = x.shape[0]
    tm = min(256, n)
    out = jax.ShapeDtypeStruct((n, ATTN_WIDTH), F32)
    return pl.pallas_call(
        _inproj_kernel,
        grid=(n // tm,),
        in_specs=[pl.BlockSpec((tm, D_MODEL), lambda i: (i, 0)),
                  pl.BlockSpec((1, D_MODEL), lambda i: (0, 0)),
                  pl.BlockSpec((D_MODEL, IN_SPLITS * ATTN_WIDTH), lambda i: (0, 0))],
        out_specs=[pl.BlockSpec((tm, ATTN_WIDTH), lambda i: (i, 0))] * IN_SPLITS,
        out_shape=[out] * IN_SPLITS,
        compiler_params=_params("parallel"),
    )(x, g.reshape(1, D_MODEL), w)


def _merge_groups(outs, lses):
    m = functools.reduce(jnp.maximum, lses)
    es = [jnp.exp(l - m) for l in lses]
    den = functools.reduce(jnp.add, es)
    return functools.reduce(jnp.add, [(e / den) * o for e, o in zip(es, outs)])


def _prompt_attn_kernel(s_len, q_ref, k_ref, v_ref, slope_ref, o_ref, og_ref, lg_ref):
    lane = lax.broadcasted_iota(jnp.int32, (ATTN_BLOCK, LANES), 1)
    first = lane < ATTN_HEAD_DIM
    qi = lax.broadcasted_iota(jnp.int32, (ATTN_BLOCK, 2 * ATTN_BLOCK), 0)
    kj = lax.broadcasted_iota(jnp.int32, (ATTN_BLOCK, 2 * ATTN_BLOCK), 1)
    back = qi + ATTN_BLOCK - kj
    in_band = (back >= 0) & (back <= N_STEPS)
    nt = (((1,), (1,)), ((), ()))
    ntile = s_len // ATTN_BLOCK
    for g, dil in enumerate(DILATIONS):
        dist = (back * dil).astype(F32)
        bias = [jnp.where(in_band, -slope_ref[hh, 0:1, 0:1] * dist, -jnp.inf) for hh in range(2)]

        def tile(t, carry, g=g, dil=dil, bias=bias):
            r, n = t % dil, t // dil
            qrow = r + n * (dil * ATTN_BLOCK)
            krow = ATTN_PAD + qrow - dil * ATTN_BLOCK
            q = q_ref[pl.ds(qrow, ATTN_BLOCK, stride=dil), :]
            kb = k_ref[pl.ds(krow, 2 * ATTN_BLOCK, stride=dil), :].astype(MXU_DTYPE)
            vb = v_ref[pl.ds(krow, 2 * ATTN_BLOCK, stride=dil), :].astype(MXU_DTYPE)
            exists = kj + n * ATTN_BLOCK >= ATTN_BLOCK
            outs, lses = [], []
            for hh in range(2):
                qm = jnp.where(first if hh == 0 else ~first, q, 0.0).astype(MXU_DTYPE)
                sc = lax.dot_general(qm, kb, nt, preferred_element_type=F32) * ATTN_SCALE + bias[hh]
                sc = jnp.where(exists, sc, -jnp.inf)
                m = jnp.max(sc, -1, keepdims=True)
                p = jnp.exp(sc - m)
                den = jnp.sum(p, -1, keepdims=True)
                outs.append(jnp.dot((p / den).astype(MXU_DTYPE), vb, preferred_element_type=F32))
                lses.append(jnp.broadcast_to(m + jnp.log(den), (ATTN_BLOCK, LANES)))
            og_ref[g, pl.ds(qrow, ATTN_BLOCK, stride=dil), :] = jnp.where(first, outs[0], outs[1])
            lg_ref[g, pl.ds(qrow, ATTN_BLOCK, stride=dil), :] = jnp.where(first, lses[0], lses[1])
            return carry

        lax.fori_loop(0, ntile, tile, 0, unroll=2)

    def merge(i, carry):
        rows = pl.ds(pl.multiple_of(i * ATTN_BLOCK, ATTN_BLOCK), ATTN_BLOCK)
        groups = range(len(DILATIONS))
        o_ref[rows, :] = _merge_groups([og_ref[g, rows, :] for g in groups], [lg_ref[g, rows, :] for g in groups])
        return carry

    lax.fori_loop(0, ntile, merge, 0)


def _prompt_attention(q, k, v, slopes_tile):
    b, s, _ = q.shape
    pad = lambda a: jnp.pad(a, ((0, 0), (ATTN_PAD, 0), (0, 0)))
    q_spec = pl.BlockSpec((None, s, LANES), lambda i, h: (i, 0, h))
    kv_spec = pl.BlockSpec((None, ATTN_PAD + s, LANES), lambda i, h: (i, 0, h))
    return pl.pallas_call(
        functools.partial(_prompt_attn_kernel, s),
        grid=(b, ATTN_WIDTH // LANES),
        in_specs=[q_spec, kv_spec, kv_spec, pl.BlockSpec((None, 2, 8, LANES), lambda i, h: (h, 0, 0, 0))],
        out_specs=q_spec,
        out_shape=jax.ShapeDtypeStruct((b, s, ATTN_WIDTH), F32),
        scratch_shapes=[pltpu.VMEM((len(DILATIONS), s, LANES), F32), pltpu.VMEM((len(DILATIONS), s, LANES), F32)],
        compiler_params=_params("parallel", "parallel"),
    )(q, pad(k), pad(v), slopes_tile)


def _sample_attn_kernel(q_ref, kn_ref, vn_ref, slope_ref, *refs):
    kv_refs, o_ref = refs[:6], refs[6]
    head = lax.broadcasted_iota(jnp.int32, (ATTN_HEADS, ATTN_WIDTH), 0)
    lane = lax.broadcasted_iota(jnp.int32, (ATTN_HEADS, ATTN_WIDTH), 1)
    head_mask = (lane // ATTN_HEAD_DIM == head).astype(F32)
    q_exp = q_ref[...] * head_mask
    slope = slope_ref[...]
    steps_back = (N_STEPS - lax.broadcasted_iota(jnp.int32, (ATTN_HEADS, N_STEPS), 1)).astype(F32)
    l0 = jnp.sum(q_exp * kn_ref[...], -1, keepdims=True) * ATTN_SCALE
    nt = (((1,), (1,)), ((), ()))
    outs, lses = [], []
    for g, dil in enumerate(DILATIONS):
        kc = kv_refs[2 * g][...].astype(MXU_DTYPE)
        vc = kv_refs[2 * g + 1][...].astype(MXU_DTYPE)
        sc = lax.dot_general(q_exp.astype(MXU_DTYPE), kc, nt, preferred_element_type=F32)
        sc = sc * ATTN_SCALE - slope * (steps_back * float(dil))
        m = jnp.maximum(jnp.max(sc, -1, keepdims=True), l0)
        p = jnp.exp(sc - m)
        p0 = jnp.exp(l0 - m)
        s = jnp.sum(p, -1, keepdims=True) + p0
        outs.append(jnp.dot((p / s).astype(MXU_DTYPE), vc, preferred_element_type=F32) + (p0 / s) * vn_ref[...])
        lses.append(m + jnp.log(s))
    o_ref[...] = jnp.sum(_merge_groups(outs, lses) * head_mask, 0, keepdims=True)


def _sample_attention(q, kn, vn, cache_k, cache_v, slopes_rows):
    b, w = cache_k.shape[0], cache_k.shape[1]
    row = lambda a: a.reshape(b, 1, ATTN_WIDTH)
    row_spec = pl.BlockSpec((None, 1, ATTN_WIDTH), lambda i: (i, 0, 0))
    kv, kv_specs = [], []
    for dil in DILATIONS:
        last = w // dil // N_STEPS - 1
        for c in (cache_k, cache_v):
            kv.append(c.reshape(b, w // dil, dil * ATTN_WIDTH))
            kv_specs.append(pl.BlockSpec((None, N_STEPS, ATTN_WIDTH), lambda i, last=last: (i, last, 0)))
    out = pl.pallas_call(
        _sample_attn_kernel,
        grid=(b,),
        in_specs=[row_spec, row_spec, row_spec, pl.BlockSpec((ATTN_HEADS, N_STEPS), lambda i: (0, 0))] + kv_specs,
        out_specs=row_spec,
        out_shape=jax.ShapeDtypeStruct((b, 1, ATTN_WIDTH), F32),
        compiler_params=_params("parallel"),
    )(row(q), row(kn), row(vn), slopes_rows, *kv)
    return out.reshape(b, ATTN_WIDTH)


def _ret_prompt_kernel(nchunk, q_ref, k_ref, v_ref, dmat_ref, qdec_ref, kdec_ref, cdec_ref, o_ref, s_ref):
    s_ref[...] = jnp.zeros_like(s_ref)
    nt = (((1,), (1,)), ((), ()))
    tn = (((0,), (0,)), ((), ()))

    def chunk(c, carry):
        row = pl.multiple_of(c * RET_CHUNK, RET_CHUNK)
        q = q_ref[pl.ds(row, RET_CHUNK), :]
        k = k_ref[pl.ds(row, RET_CHUNK), :]
        v = v_ref[pl.ds(row, RET_CHUNK), :].astype(MXU_DTYPE)
        qb = q.astype(MXU_DTYPE)
        state = s_ref[...]
        inner = lax.dot_general(qb, k.astype(MXU_DTYPE), nt, preferred_element_type=F32) * dmat_ref[...]
        o_in = jnp.dot(inner.astype(MXU_DTYPE), v, preferred_element_type=F32)
        o_x = jnp.dot(qb, state.astype(MXU_DTYPE), preferred_element_type=F32) * qdec_ref[...]
        o_ref[pl.ds(row, RET_CHUNK), :] = o_in + o_x
        kd = (k * kdec_ref[...]).astype(MXU_DTYPE)
        s_ref[...] = cdec_ref[0:1, :] * state + lax.dot_general(kd, v, tn, preferred_element_type=F32)
        return carry

    lax.fori_loop(0, nchunk, chunk, 0)


def _ret_prompt(q, k, v, consts):
    b, s, _ = q.shape
    seq = pl.BlockSpec((None, s, RET_HEAD_DIM), lambda i, h: (i, 0, h))
    per_head = lambda rows: pl.BlockSpec((None, rows, RET_HEAD_DIM), lambda i, h: (h, 0, 0))
    return pl.pallas_call(
        functools.partial(_ret_prompt_kernel, s // RET_CHUNK),
        grid=(b, RET_HEADS),
        in_specs=[seq, seq, seq, per_head(RET_CHUNK), per_head(RET_CHUNK), per_head(RET_CHUNK), per_head(8)],
        out_specs=[seq, pl.BlockSpec((None, None, RET_HEAD_DIM, RET_HEAD_DIM), lambda i, h: (i, h, 0, 0))],
        out_shape=[jax.ShapeDtypeStruct((b, s, RET_WIDTH), F32),
                   jax.ShapeDtypeStruct((b, RET_HEADS, RET_HEAD_DIM, RET_HEAD_DIM), F32)],
        compiler_params=_params("parallel", "parallel"),
    )(q, k, v, *consts)


def _ret_sample_kernel(qc_ref, kc_ref, v_ref, s_ref, gam_ref, o_ref, sn_ref):
    for h in range(RET_HEADS):
        qc = qc_ref[h]
        kc = kc_ref[h]
        v = v_ref[:, h * RET_HEAD_DIM:(h + 1) * RET_HEAD_DIM]
        gam = gam_ref[h, 0:1, :]
        state = s_ref[h]
        inner = jnp.sum(qc * kc, 0, keepdims=True)
        o_x = jnp.sum(qc * state, 0, keepdims=True)
        o_ref[:, h * RET_HEAD_DIM:(h + 1) * RET_HEAD_DIM] = inner * v + gam * o_x
        sn_ref[h] = gam * state + kc * v


def _ret_sample(q, k, v, state, gam_tile):
    b = q.shape[0]
    col = lambda a: a.reshape(b, RET_HEADS, RET_HEAD_DIM, 1)
    col_spec = pl.BlockSpec((None, RET_HEADS, RET_HEAD_DIM, 1), lambda i: (i, 0, 0, 0))
    row_spec = pl.BlockSpec((None, 1, RET_WIDTH), lambda i: (i, 0, 0))
    st_spec = pl.BlockSpec((None, RET_HEADS, RET_HEAD_DIM, RET_HEAD_DIM), lambda i: (i, 0, 0, 0))
    o, sn = pl.pallas_call(
        _ret_sample_kernel,
        grid=(b,),
        in_specs=[col_spec, col_spec, row_spec, st_spec,
                  pl.BlockSpec((RET_HEADS, 8, RET_HEAD_DIM), lambda i: (0, 0, 0))],
        out_specs=[row_spec, st_spec],
        out_shape=[jax.ShapeDtypeStruct((b, 1, RET_WIDTH), F32), jax.ShapeDtypeStruct(state.shape, F32)],
        compiler_params=_params("parallel"),
    )(col(q), col(k), v.reshape(b, 1, RET_WIDTH), state, gam_tile)
    return o.reshape(b, RET_WIDTH), sn


def _mixer_out_kernel(attn_ref, ret_ref, gate_ref, x_ref, gn_ref, wout_ref, n2_ref, wpq_ref, keys_ref,
                      h_ref, hn_ref, st_ref):
    gate = gate_ref[...]
    swish = gate * (1.0 / (1.0 + jnp.exp(-gate)))
    normed = []
    for hd in range(RET_HEADS):
        cols = slice(hd * RET_HEAD_DIM, (hd + 1) * RET_HEAD_DIM)
        r = ret_ref[:, cols]
        mu = jnp.mean(r, -1, keepdims=True)
        var = jnp.mean(jnp.square(r - mu), -1, keepdims=True)
        normed.append((r - mu) * lax.rsqrt(var + GN_EPS) * gn_ref[:, cols])
    ret_y = swish * jnp.concatenate(normed, -1)
    h = (x_ref[...]
         + jnp.dot(attn_ref[...].astype(MXU_DTYPE), wout_ref[0:ATTN_WIDTH, :], preferred_element_type=F32)
         + jnp.dot(ret_y.astype(MXU_DTYPE), wout_ref[ATTN_WIDTH:, :], preferred_element_type=F32))
    h_ref[...] = h
    hn = _rms(h, n2_ref[...])
    hn_ref[...] = hn
    hb = hn.astype(MXU_DTYPE)
    nt = (((1,), (1,)), ((), ()))
    for j in range(2 * PEER_HEADS):
        qj = jnp.dot(hb, wpq_ref[:, j * PEER_HALF:(j + 1) * PEER_HALF], preferred_element_type=F32)
        st_ref[j] = lax.dot_general(keys_ref[j], qj.astype(MXU_DTYPE), nt, preferred_element_type=F32)


def _mixer_out(attn, ret_o, gate, x, gn_g, w_out, n2_g, w_pq, keys):
    n = x.shape[0]
    tm = min(256, n)
    half = pl.BlockSpec((tm, ATTN_WIDTH), lambda i: (i, 0))
    full = pl.BlockSpec((tm, D_MODEL), lambda i: (i, 0))
    const = lambda shape: pl.BlockSpec(shape, lambda i: (0,) * len(shape))
    return pl.pallas_call(
        _mixer_out_kernel,
        grid=(n // tm,),
        in_specs=[half] * 3 + [full, const((1, RET_WIDTH)), const((D_MODEL, D_MODEL)), const((1, D_MODEL)),
                               const((D_MODEL, 2 * PEER_HEADS * PEER_HALF)),
                               const((2 * PEER_HEADS, PEER_N_KEYS, PEER_HALF))],
        out_specs=[full, full, pl.BlockSpec((2 * PEER_HEADS, PEER_N_KEYS, tm), lambda i: (0, 0, i))],
        out_shape=[jax.ShapeDtypeStruct((n, D_MODEL), F32), jax.ShapeDtypeStruct((n, D_MODEL), F32),
                   jax.ShapeDtypeStruct((2 * PEER_HEADS, PEER_N_KEYS, n), F32)],
        compiler_params=_params("parallel"),
    )(attn, ret_o, gate, x, gn_g.reshape(1, RET_WIDTH), w_out, n2_g.reshape(1, D_MODEL), w_pq, keys)


def _top_rows(s, k, payload=None):
    nrow = s.shape[0]
    ridx = lax.broadcasted_iota(jnp.int32, s.shape, 0)
    vals, picks = [], []
    for _ in range(k):
        m = jnp.max(s, 0, keepdims=True)
        am = jnp.min(jnp.where(s == m, ridx, nrow), 0, keepdims=True)
        hit = ridx == am
        vals.append(m)
        picks.append(am if payload is None else jnp.sum(jnp.where(hit, payload, 0), 0, keepdims=True))
        s = jnp.where(hit, -jnp.inf, s)
    return jnp.concatenate(vals, 0), jnp.concatenate(picks, 0)


def _product_grid(a0, a1, op):
    pieces = [op(a0[0:1], a1)]
    pieces += [op(a0[a:a + 1], a1[0:8]) for a in range(1, 8)]
    pieces.append(op(a0[8:16], a1[0:1]))
    return jnp.concatenate(pieces, 0)


def _topk_kernel(st_ref, eid_ref, g_ref):
    sv0, si0 = _top_rows(st_ref[0], PEER_TOPK)
    sv1, si1 = _top_rows(st_ref[1], PEER_TOPK)
    cand = _product_grid(sv0, sv1, jnp.add)
    cidx = _product_grid(si0, si1, lambda i, j: i * PEER_N_KEYS + j)
    best, eid = _top_rows(cand, PEER_TOPK, payload=cidx)
    e = jnp.exp(best - jnp.max(best, 0, keepdims=True))
    eid_ref[...] = eid
    g_ref[...] = e / jnp.sum(e, 0, keepdims=True)


def _topk(st):
    n = st.shape[-1]
    tn = LANES
    out_spec = pl.BlockSpec((PEER_TOPK, tn), lambda i, h: (h, i))
    return pl.pallas_call(
        _topk_kernel,
        grid=(n // tn, PEER_HEADS),
        in_specs=[pl.BlockSpec((2, PEER_N_KEYS, tn), lambda i, h: (h, 0, i))],
        out_specs=[out_spec, out_spec],
        out_shape=[jax.ShapeDtypeStruct((PEER_SLOTS, n), jnp.int32), jax.ShapeDtypeStruct((PEER_SLOTS, n), F32)],
        compiler_params=_params("parallel", "parallel"),
    )(st)


def _peer_kernel(eid_hbm, gt_ref, hn_ref, h_ref, nf_ref, uv_hbm, y_ref, eid_smem, buf, sems):
    step = pl.program_id(0)
    nchunk = PEER_TOKENS_PER_STEP // PEER_CHUNK
    rows = PEER_CHUNK * PEER_SLOTS
    ids = pltpu.make_async_copy(eid_hbm.at[step], eid_smem, sems.at[2])
    ids.start()
    ids.wait()

    def gather(c, slot):
        def one(i, carry):
            e = eid_smem[c * rows + i]
            pltpu.make_async_copy(uv_hbm.at[pl.ds(e, 1)], buf.at[slot, pl.ds(i, 1)], sems.at[slot]).start()
            return carry
        lax.fori_loop(0, rows, one, 0, unroll=8)

    def wait(slot):
        pltpu.make_async_copy(uv_hbm.at[pl.ds(0, rows)], buf.at[slot], sems.at[slot]).wait()

    tok_lane = lax.broadcasted_iota(jnp.int32, (PEER_SLOTS, PEER_TOKENS_PER_STEP), 1)
    gather(0, 0)

    def chunk(c, carry):
        slot = c % 2

        @pl.when(c + 1 < nchunk)
        def _():
            gather(c + 1, 1 - slot)

        wait(slot)
        for t in range(PEER_CHUNK):
            tok = c * PEER_CHUNK + t
            x = hn_ref[pl.ds(tok, 1), :]
            u = buf[slot, t * PEER_SLOTS:(t + 1) * PEER_SLOTS, 0:D_MODEL]
            a = jax.nn.gelu(jnp.sum(u * x, -1, keepdims=True))
            gate = jnp.sum(jnp.where(tok_lane == tok, gt_ref[...], 0.0), -1, keepdims=True)
            v = buf[slot, t * PEER_SLOTS:(t + 1) * PEER_SLOTS, D_MODEL:2 * D_MODEL]
            out = jnp.sum((gate * a) * v, 0, keepdims=True)
            y_ref[pl.ds(tok, 1), :] = _rms(h_ref[pl.ds(tok, 1), :] + out, nf_ref[...])
        return carry

    lax.fori_loop(0, nchunk, chunk, 0)


def _peer(eid_t, g_t, hn, h, nf_g, uv):
    n = hn.shape[0]
    tp = PEER_TOKENS_PER_STEP
    eid_tok = eid_t.T.reshape(n // tp, tp * PEER_SLOTS)
    rows = PEER_CHUNK * PEER_SLOTS
    tok = pl.BlockSpec((tp, D_MODEL), lambda i: (i, 0))
    return pl.pallas_call(
        _peer_kernel,
        grid=(n // tp,),
        in_specs=[pl.BlockSpec(memory_space=pl.ANY),
                  pl.BlockSpec((PEER_SLOTS, tp), lambda i: (0, i)),
                  tok, tok,
                  pl.BlockSpec((1, D_MODEL), lambda i: (0, 0)),
                  pl.BlockSpec(memory_space=pl.ANY)],
        out_specs=tok,
        out_shape=jax.ShapeDtypeStruct((n, D_MODEL), F32),
        scratch_shapes=[pltpu.SMEM((tp * PEER_SLOTS,), jnp.int32),
                        pltpu.VMEM((2, rows, 2 * D_MODEL), F32),
                        pltpu.SemaphoreType.DMA((3,))],
        compiler_params=_params("arbitrary"),
    )(eid_tok, g_t, hn, h, nf_g.reshape(1, D_MODEL), uv)


def _alibi_slopes():
    return (2.0 ** (-8.0 * (np.arange(ATTN_HEADS) + 1) / ATTN_HEADS)).astype(np.float32)


def _ret_consts():
    log_gamma = np.log(1.0 - 2.0 ** (-5.0 - np.arange(RET_HEADS))).astype(np.float32)
    pos = np.arange(RET_CHUNK, dtype=np.float32)
    rel = pos[:, None] - pos[None, :]
    dmat = np.where(rel[None] >= 0, np.exp(np.maximum(rel, 0.0)[None] * log_gamma[:, None, None]), 0.0)
    lanes = lambda col: np.broadcast_to(col[:, :, None], (RET_HEADS, col.shape[1], RET_HEAD_DIM))
    qdec = lanes(np.exp((pos[None, :] + 1.0) * log_gamma[:, None]))
    kdec = lanes(np.exp((RET_CHUNK - 1.0 - pos)[None, :] * log_gamma[:, None]))
    cdec = lanes(np.broadcast_to(np.exp(RET_CHUNK * log_gamma)[:, None], (RET_HEADS, 8)))
    gam = lanes(np.broadcast_to(np.exp(log_gamma)[:, None], (RET_HEADS, 8)))
    as_f32 = lambda a: jnp.asarray(np.ascontiguousarray(a, dtype=np.float32))
    return tuple(as_f32(a) for a in (dmat, qdec, kdec, cdec)), as_f32(gam)


def _channel_and_norm(attn, ret_o, gate, x, gn_g, w_out, n2_g, w_pq, keys, uv, nf_g):
    h, hn, st = _mixer_out(attn, ret_o, gate, x, gn_g, w_out, n2_g, w_pq, keys)
    eid_t, g_t = _topk(st)
    return _peer(eid_t, g_t, hn, h, nf_g, uv)


def kernel(x_prompt, x_sample, cache_k_win, cache_v_win, state_ret, norm1_g, w_in, ret_gn_g, w_out, norm2_g, w_pq, peer_sub_keys, peer_u, peer_v, norm_f_g):
    depth = w_in.shape[0]
    assert depth == 1, "single-layer stack"
    b, s, _ = x_prompt.shape
    bs = x_sample.shape[0]
    assert x_sample.shape[1] == 1 and s % (ATTN_BLOCK * max(DILATIONS)) == 0
    assert cache_k_win.shape[2] % (N_STEPS * max(DILATIONS)) == 0

    slopes = _alibi_slopes()
    slopes_tile = jnp.asarray(np.broadcast_to(slopes.reshape(ATTN_HEADS // 2, 2, 1, 1), (ATTN_HEADS // 2, 2, 8, LANES)).copy())
    slopes_rows = jnp.asarray(np.broadcast_to(slopes[:, None], (ATTN_HEADS, N_STEPS)).copy())
    ret_consts, gam_tile = _ret_consts()

    w_in_b = w_in[0].astype(MXU_DTYPE)
    w_out_b = w_out[0].astype(MXU_DTYPE)
    w_pq_b = w_pq[0].astype(MXU_DTYPE)
    keys_b = peer_sub_keys[0].reshape(2 * PEER_HEADS, PEER_N_KEYS, PEER_HALF).astype(MXU_DTYPE)
    uv = jnp.concatenate([peer_u[0], peer_v[0]], axis=1)
    tail = (ret_gn_g[0], w_out_b, norm2_g[0], w_pq_b, keys_b, uv, norm_f_g)

    xp = x_prompt.reshape(b * s, D_MODEL)
    aq, ak, av, rq, rk, rv, rg = _inproj(xp, norm1_g[0], w_in_b)
    seq = lambda a: a.reshape(b, s, ATTN_WIDTH)
    attn = _prompt_attention(seq(aq), seq(ak), seq(av), slopes_tile)
    ret_o, s_fin = _ret_prompt(seq(rq), seq(rk), seq(rv), ret_consts)
    y_prompt = _channel_and_norm(attn.reshape(b * s, ATTN_WIDTH), ret_o.reshape(b * s, RET_WIDTH), rg, xp, *tail)
    keep = min(N_STEPS * max(DILATIONS), s)
    k_win = ak.reshape(b, s, ATTN_HEADS, ATTN_HEAD_DIM)[:, s - keep:]
    v_win = av.reshape(b, s, ATTN_HEADS, ATTN_HEAD_DIM)[:, s - keep:]

    xs = x_sample.reshape(bs, D_MODEL)
    aq, ak, av, rq, rk, rv, rg = _inproj(xs, norm1_g[0], w_in_b)
    attn = _sample_attention(aq, ak, av, cache_k_win[0], cache_v_win[0], slopes_rows)
    ret_o, s_new = _ret_sample(rq, rk, rv, state_ret[0], gam_tile)
    y_sample = _channel_and_norm(attn, ret_o, rg, xs, *tail)

    return (y_prompt.reshape(b, s, D_MODEL), y_sample.reshape(bs, 1, D_MODEL),
            k_win[None], v_win[None], s_fin[None],
            ak.reshape(1, bs, 1, ATTN_HEADS, ATTN_HEAD_DIM), av.reshape(1, bs, 1, ATTN_HEADS, ATTN_HEAD_DIM),
            s_new[None])
```

```python
import functools

import numpy as np
import jax
import jax.numpy as jnp
from jax import lax
from jax.experimental import pallas as pl
from jax.experimental.pallas import tpu as pltpu

F32 = jnp.float32
MXU_DTYPE = jnp.bfloat16

D_MODEL = 1024
ATTN_WIDTH = 512
RET_WIDTH = 512
ATTN_HEAD_DIM = 64
ATTN_HEADS = 8
RET_HEADS = 4
RET_HEAD_DIM = 128
DILATIONS = (1, 4, 16)
N_STEPS = 128
ATTN_BLOCK = 128
ATTN_PAD = N_STEPS * max(DILATIONS)
RET_CHUNK = 128
PEER_HEADS = 8
PEER_N_KEYS = 128
PEER_TOPK = 16
PEER_HALF = 128
PEER_SLOTS = PEER_HEADS * PEER_TOPK
NORM_EPS = 1e-6
GN_EPS = 1e-5
IN_SPLITS = 7
RET_K_SCALE = RET_HEAD_DIM ** -0.5
ATTN_SCALE = ATTN_HEAD_DIM ** -0.5
LANES = 128
VMEM_LIMIT = 48 * 1024 * 1024

PEER_TOKENS_PER_STEP = 128
PEER_CHUNK = 8
PEER_SLAB = 2 * D_MODEL // LANES
PEER_PITCH = PEER_SLAB + 1
PEER_ISSUE_UNROLL = 8


def _params(*sem):
    return pltpu.CompilerParams(dimension_semantics=sem, vmem_limit_bytes=VMEM_LIMIT)


def _rms(x, g):
    r = lax.rsqrt(jnp.mean(x * x, axis=-1, keepdims=True) + NORM_EPS)
    return x * r * g


def _inproj_kernel(x_ref, g_ref, w_ref, *out_refs):
    xn = _rms(x_ref[...], g_ref[...]).astype(MXU_DTYPE)
    for i, o_ref in enumerate(out_refs):
        y = jnp.dot(xn, w_ref[:, i * ATTN_WIDTH:(i + 1) * ATTN_WIDTH], preferred_element_type=F32)
        if i == 4:
            y = y * RET_K_SCALE
        o_ref[...] = y


def _inproj(x, g, w):
    n = x.shape[0]
    tm = min(256, n)
    out = jax.ShapeDtypeStruct((n, ATTN_WIDTH), F32)
    return pl.pallas_call(
        _inproj_kernel,
        grid=(n // tm,),
        in_specs=[pl.BlockSpec((tm, D_MODEL), lambda i: (i, 0)),
                  pl.BlockSpec((1, D_MODEL), lambda i: (0, 0)),
                  pl.BlockSpec((D_MODEL, IN_SPLITS * ATTN_WIDTH), lambda i: (0, 0))],
        out_specs=[pl.BlockSpec((tm, ATTN_WIDTH), lambda i: (i, 0))] * IN_SPLITS,
        out_shape=[out] * IN_SPLITS,
        compiler_params=_params("parallel"),
    )(x, g.reshape(1, D_MODEL), w)


def _merge_groups(outs, lses):
    m = functools.reduce(jnp.maximum, lses)
    es = [jnp.exp(l - m) for l in lses]
    den = functools.reduce(jnp.add, es)
    return functools.reduce(jnp.add, [(e / den) * o for e, o in zip(es, outs)])


def _prompt_attn_kernel(s_len, q_ref, k_ref, v_ref, slope_ref, o_ref, og_ref, lg_ref):
    lane = lax.broadcasted_iota(jnp.int32, (ATTN_BLOCK, LANES), 1)
    first = lane < ATTN_HEAD_DIM
    qi = lax.broadcasted_iota(jnp.int32, (ATTN_BLOCK, 2 * ATTN_BLOCK), 0)
    kj = lax.broadcasted_iota(jnp.int32, (ATTN_BLOCK, 2 * ATTN_BLOCK), 1)
    back = qi + ATTN_BLOCK - kj
    in_band = (back >= 0) & (back <= N_STEPS)
    nt = (((1,), (1,)), ((), ()))
    ntile = s_len // ATTN_BLOCK
    for g, dil in enumerate(DILATIONS):
        dist = (back * dil).astype(F32)
        bias = [jnp.where(in_band, -slope_ref[hh, 0:1, 0:1] * dist, -jnp.inf) for hh in range(2)]

        def tile(t, carry, g=g, dil=dil, bias=bias):
            r, n = t % dil, t // dil
            qrow = r + n * (dil * ATTN_BLOCK)
            krow = ATTN_PAD + qrow - dil * ATTN_BLOCK
            q = q_ref[pl.ds(qrow, ATTN_BLOCK, stride=dil), :]
            kb = k_ref[pl.ds(krow, 2 * ATTN_BLOCK, stride=dil), :].astype(MXU_DTYPE)
            vb = v_ref[pl.ds(krow, 2 * ATTN_BLOCK, stride=dil), :].astype(MXU_DTYPE)
            exists = kj + n * ATTN_BLOCK >= ATTN_BLOCK
            outs, lses = [], []
            for hh in range(2):
                qm = jnp.where(first if hh == 0 else ~first, q, 0.0).astype(MXU_DTYPE)
                sc = lax.dot_general(qm, kb, nt, preferred_element_type=F32) * ATTN_SCALE + bias[hh]
                sc = jnp.where(exists, sc, -jnp.inf)
                m = jnp.max(sc, -1, keepdims=True)
                p = jnp.exp(sc - m)
                den = jnp.sum(p, -1, keepdims=True)
                outs.append(jnp.dot((p / den).astype(MXU_DTYPE), vb, preferred_element_type=F32))
                lses.append(jnp.broadcast_to(m + jnp.log(den), (ATTN_BLOCK, LANES)))
            og_ref[g, pl.ds(qrow, ATTN_BLOCK, stride=dil), :] = jnp.where(first, outs[0], outs[1])
            lg_ref[g, pl.ds(qrow, ATTN_BLOCK, stride=dil), :] = jnp.where(first, lses[0], lses[1])
            return carry

        lax.fori_loop(0, ntile, tile, 0, unroll=2)

    def merge(i, carry):
        rows = pl.ds(pl.multiple_of(i * ATTN_BLOCK, ATTN_BLOCK), ATTN_BLOCK)
        groups = range(len(DILATIONS))
        o_ref[rows, :] = _merge_groups([og_ref[g, rows, :] for g in groups], [lg_ref[g, rows, :] for g in groups])
        return carry

    lax.fori_loop(0, ntile, merge, 0)


def _prompt_attention(q, k, v, slopes_tile):
    b, s, _ = q.shape
    pad = lambda a: jnp.pad(a, ((0, 0), (ATTN_PAD, 0), (0, 0)))
    q_spec = pl.BlockSpec((None, s, LANES), lambda i, h: (i, 0, h))
    kv_spec = pl.BlockSpec((None, ATTN_PAD + s, LANES), lambda i, h: (i, 0, h))
    return pl.pallas_call(
        functools.partial(_prompt_attn_kernel, s),
        grid=(b, ATTN_WIDTH // LANES),
        in_specs=[q_spec, kv_spec, kv_spec, pl.BlockSpec((None, 2, 8, LANES), lambda i, h: (h, 0, 0, 0))],
        out_specs=q_spec,
        out_shape=jax.ShapeDtypeStruct((b, s, ATTN_WIDTH), F32),
        scratch_shapes=[pltpu.VMEM((len(DILATIONS), s, LANES), F32), pltpu.VMEM((len(DILATIONS), s, LANES), F32)],
        compiler_params=_params("parallel", "parallel"),
    )(q, pad(k), pad(v), slopes_tile)


def _sample_attn_kernel(q_ref, kn_ref, vn_ref, slope_ref, *refs):
    kv_refs, o_ref = refs[:6], refs[6]
    q = q_ref[...]
    slope = slope_ref[:, 0:1]
    steps_back = (N_STEPS - lax.broadcasted_iota(jnp.int32, (N_STEPS, ATTN_HEADS, 1), 0)).astype(F32)
    l0 = jnp.sum(q * kn_ref[...], -1, keepdims=True) * ATTN_SCALE
    outs, lses = [], []
    for g, dil in enumerate(DILATIONS):
        kc = kv_refs[2 * g][...]
        vc = kv_refs[2 * g + 1][...]
        sc = jnp.sum(kc * q, -1, keepdims=True) * ATTN_SCALE - slope * (steps_back * float(dil))
        m = jnp.maximum(jnp.max(sc, 0), l0)
        p = jnp.exp(sc - m)
        p0 = jnp.exp(l0 - m)
        s = jnp.sum(p, 0) + p0
        outs.append((jnp.sum(p * vc, 0) + p0 * vn_ref[...]) / s)
        lses.append(m + jnp.log(s))
    o_ref[...] = _merge_groups(outs, lses)


def _sample_attention(q, kn, vn, cache_k, cache_v, slopes_heads):
    b, w = cache_k.shape[0], cache_k.shape[1]
    heads = lambda a: a.reshape(b, ATTN_HEADS, ATTN_HEAD_DIM)
    head_spec = pl.BlockSpec((None, ATTN_HEADS, ATTN_HEAD_DIM), lambda i: (i, 0, 0))
    kv, kv_specs = [], []
    for dil in DILATIONS:
        last = w // dil // N_STEPS - 1
        for c in (cache_k, cache_v):
            kv.append(c.reshape(b, w // dil, dil, ATTN_HEADS, ATTN_HEAD_DIM))
            kv_specs.append(pl.BlockSpec((None, N_STEPS, None, ATTN_HEADS, ATTN_HEAD_DIM),
                                         lambda i, last=last: (i, last, 0, 0, 0)))
    out = pl.pallas_call(
        _sample_attn_kernel,
        grid=(b,),
        in_specs=[head_spec, head_spec, head_spec,
                  pl.BlockSpec((ATTN_HEADS, ATTN_HEAD_DIM), lambda i: (0, 0))] + kv_specs,
        out_specs=head_spec,
        out_shape=jax.ShapeDtypeStruct((b, ATTN_HEADS, ATTN_HEAD_DIM), F32),
        compiler_params=_params("parallel"),
    )(heads(q), heads(kn), heads(vn), slopes_heads, *kv)
    return out.reshape(b, ATTN_WIDTH)


def _ret_prompt_kernel(nchunk, q_ref, k_ref, v_ref, dmat_ref, qdec_ref, kdec_ref, cdec_ref, o_ref, s_ref):
    s_ref[...] = jnp.zeros_like(s_ref)
    nt = (((1,), (1,)), ((), ()))
    tn = (((0,), (0,)), ((), ()))

    def chunk(c, carry):
        row = pl.multiple_of(c * RET_CHUNK, RET_CHUNK)
        q = q_ref[pl.ds(row, RET_CHUNK), :]
        k = k_ref[pl.ds(row, RET_CHUNK), :]
        v = v_ref[pl.ds(row, RET_CHUNK), :].astype(MXU_DTYPE)
        qb = q.astype(MXU_DTYPE)
        state = s_ref[...]
        inner = lax.dot_general(qb, k.astype(MXU_DTYPE), nt, preferred_element_type=F32) * dmat_ref[...]
        o_in = jnp.dot(inner.astype(MXU_DTYPE), v, preferred_element_type=F32)
        o_x = jnp.dot(qb, state.astype(MXU_DTYPE), preferred_element_type=F32) * qdec_ref[...]
        o_ref[pl.ds(row, RET_CHUNK), :] = o_in + o_x
        kd = (k * kdec_ref[...]).astype(MXU_DTYPE)
        s_ref[...] = cdec_ref[0:1, :] * state + lax.dot_general(kd, v, tn, preferred_element_type=F32)
        return carry

    lax.fori_loop(0, nchunk, chunk, 0)


def _ret_prompt(q, k, v, consts):
    b, s, _ = q.shape
    seq = pl.BlockSpec((None, s, RET_HEAD_DIM), lambda i, h: (i, 0, h))
    per_head = lambda rows: pl.BlockSpec((None, rows, RET_HEAD_DIM), lambda i, h: (h, 0, 0))
    return pl.pallas_call(
        functools.partial(_ret_prompt_kernel, s // RET_CHUNK),
        grid=(b, RET_HEADS),
        in_specs=[seq, seq, seq, per_head(RET_CHUNK), per_head(RET_CHUNK), per_head(RET_CHUNK), per_head(8)],
        out_specs=[seq, pl.BlockSpec((None, None, RET_HEAD_DIM, RET_HEAD_DIM), lambda i, h: (i, h, 0, 0))],
        out_shape=[jax.ShapeDtypeStruct((b, s, RET_WIDTH), F32),
                   jax.ShapeDtypeStruct((b, RET_HEADS, RET_HEAD_DIM, RET_HEAD_DIM), F32)],
        compiler_params=_params("parallel", "parallel"),
    )(q, k, v, *consts)


def _ret_sample_kernel(qc_ref, kc_ref, v_ref, s_ref, gam_ref, o_ref, sn_ref):
    for h in range(RET_HEADS):
        qc = qc_ref[h]
        kc = kc_ref[h]
        v = v_ref[:, h * RET_HEAD_DIM:(h + 1) * RET_HEAD_DIM]
        gam = gam_ref[h, 0:1, :]
        state = s_ref[h]
        inner = jnp.sum(qc * kc, 0, keepdims=True)
        o_x = jnp.sum(qc * state, 0, keepdims=True)
        o_ref[:, h * RET_HEAD_DIM:(h + 1) * RET_HEAD_DIM] = inner * v + gam * o_x
        sn_ref[h] = gam * state + kc * v


def _ret_sample(q, k, v, state, gam_tile):
    b = q.shape[0]
    col = lambda a: a.reshape(b, RET_HEADS, RET_HEAD_DIM, 1)
    col_spec = pl.BlockSpec((None, RET_HEADS, RET_HEAD_DIM, 1), lambda i: (i, 0, 0, 0))
    row_spec = pl.BlockSpec((None, 1, RET_WIDTH), lambda i: (i, 0, 0))
    st_spec = pl.BlockSpec((None, RET_HEADS, RET_HEAD_DIM, RET_HEAD_DIM), lambda i: (i, 0, 0, 0))
    o, sn = pl.pallas_call(
        _ret_sample_kernel,
        grid=(b,),
        in_specs=[col_spec, col_spec, row_spec, st_spec,
                  pl.BlockSpec((RET_HEADS, 8, RET_HEAD_DIM), lambda i: (0, 0, 0))],
        out_specs=[row_spec, st_spec],
        out_shape=[jax.ShapeDtypeStruct((b, 1, RET_WIDTH), F32), jax.ShapeDtypeStruct(state.shape, F32)],
        compiler_params=_params("parallel"),
    )(col(q), col(k), v.reshape(b, 1, RET_WIDTH), state, gam_tile)
    return o.reshape(b, RET_WIDTH), sn


def _mixer_out_kernel(attn_ref, ret_ref, gate_ref, x_ref, gn_ref, wout_ref, n2_ref, wpq_ref, keys_ref,
                      h_ref, hn_ref, st_ref):
    gate = gate_ref[...]
    swish = gate * (1.0 / (1.0 + jnp.exp(-gate)))
    normed = []
    for hd in range(RET_HEADS):
        cols = slice(hd * RET_HEAD_DIM, (hd + 1) * RET_HEAD_DIM)
        r = ret_ref[:, cols]
        mu = jnp.mean(r, -1, keepdims=True)
        var = jnp.mean(jnp.square(r - mu), -1, keepdims=True)
        normed.append((r - mu) * lax.rsqrt(var + GN_EPS) * gn_ref[:, cols])
    ret_y = swish * jnp.concatenate(normed, -1)
    h = (x_ref[...]
         + jnp.dot(attn_ref[...].astype(MXU_DTYPE), wout_ref[0:ATTN_WIDTH, :], preferred_element_type=F32)
         + jnp.dot(ret_y.astype(MXU_DTYPE), wout_ref[ATTN_WIDTH:, :], preferred_element_type=F32))
    h_ref[...] = h
    hn = _rms(h, n2_ref[...])
    hn_ref[...] = hn
    hb = hn.astype(MXU_DTYPE)
    nt = (((1,), (1,)), ((), ()))
    for j in range(2 * PEER_HEADS):
        qj = jnp.dot(hb, wpq_ref[:, j * PEER_HALF:(j + 1) * PEER_HALF], preferred_element_type=F32)
        st_ref[j] = lax.dot_general(keys_ref[j], qj.astype(MXU_DTYPE), nt, preferred_element_type=F32)


def _mixer_out(attn, ret_o, gate, x, gn_g, w_out, n2_g, w_pq, keys):
    n = x.shape[0]
    tm = min(256, n)
    half = pl.BlockSpec((tm, ATTN_WIDTH), lambda i: (i, 0))
    full = pl.BlockSpec((tm, D_MODEL), lambda i: (i, 0))
    const = lambda shape: pl.BlockSpec(shape, lambda i: (0,) * len(shape))
    return pl.pallas_call(
        _mixer_out_kernel,
        grid=(n // tm,),
        in_specs=[half] * 3 + [full, const((1, RET_WIDTH)), const((D_MODEL, D_MODEL)), const((1, D_MODEL)),
                               const((D_MODEL, 2 * PEER_HEADS * PEER_HALF)),
                               const((2 * PEER_HEADS, PEER_N_KEYS, PEER_HALF))],
        out_specs=[full, full, pl.BlockSpec((2 * PEER_HEADS, PEER_N_KEYS, tm), lambda i: (0, 0, i))],
        out_shape=[jax.ShapeDtypeStruct((n, D_MODEL), F32), jax.ShapeDtypeStruct((n, D_MODEL), F32),
                   jax.ShapeDtypeStruct((2 * PEER_HEADS, PEER_N_KEYS, n), F32)],
        compiler_params=_params("parallel"),
    )(attn, ret_o, gate, x, gn_g.reshape(1, RET_WIDTH), w_out, n2_g.reshape(1, D_MODEL), w_pq, keys)


def _top_rows(s, k, payload=None):
    nrow = s.shape[0]
    ridx = lax.broadcasted_iota(jnp.int32, s.shape, 0)
    vals, picks = [], []
    for _ in range(k):
        m = jnp.max(s, 0, keepdims=True)
        am = jnp.min(jnp.where(s == m, ridx, nrow), 0, keepdims=True)
        hit = ridx == am
        vals.append(m)
        picks.append(am if payload is None else jnp.sum(jnp.where(hit, payload, 0), 0, keepdims=True))
        s = jnp.where(hit, -jnp.inf, s)
    return jnp.concatenate(vals, 0), jnp.concatenate(picks, 0)


def _product_grid(a0, a1, op):
    pieces = [op(a0[0:1], a1)]
    pieces += [op(a0[a:a + 1], a1[0:8]) for a in range(1, 8)]
    pieces.append(op(a0[8:16], a1[0:1]))
    return jnp.concatenate(pieces, 0)


def _topk_kernel(st_ref, eid_ref, g_ref):
    sv0, si0 = _top_rows(st_ref[0], PEER_TOPK)
    sv1, si1 = _top_rows(st_ref[1], PEER_TOPK)
    cand = _product_grid(sv0, sv1, jnp.add)
    cidx = _product_grid(si0, si1, lambda i, j: i * PEER_N_KEYS + j)
    best, eid = _top_rows(cand, PEER_TOPK, payload=cidx)
    e = jnp.exp(best - jnp.max(best, 0, keepdims=True))
    eid_ref[...] = eid
    g_ref[...] = e / jnp.sum(e, 0, keepdims=True)


def _topk(st):
    n = st.shape[-1]
    tn = LANES
    out_spec = pl.BlockSpec((PEER_TOPK, tn), lambda i, h: (h, i))
    return pl.pallas_call(
        _topk_kernel,
        grid=(n // tn, PEER_HEADS),
        in_specs=[pl.BlockSpec((2, PEER_N_KEYS, tn), lambda i, h: (h, 0, i))],
        out_specs=[out_spec, out_spec],
        out_shape=[jax.ShapeDtypeStruct((PEER_SLOTS, n), jnp.int32), jax.ShapeDtypeStruct((PEER_SLOTS, n), F32)],
        compiler_params=_params("parallel", "parallel"),
    )(st)


def _peer_kernel(eid_hbm, gt_ref, hn_ref, h_ref, nf_ref, uv_hbm, y_ref, eid_smem, buf, sems):
    step = pl.program_id(0)
    nchunk = PEER_TOKENS_PER_STEP // PEER_CHUNK
    nexp = PEER_CHUNK * PEER_SLOTS
    ids = pltpu.make_async_copy(eid_hbm.at[step], eid_smem, sems.at[2])
    ids.start()
    ids.wait()

    def gather(c, slot):
        def group(j, carry):
            for k in range(PEER_ISSUE_UNROLL):
                e = eid_smem[c * nexp + j * PEER_ISSUE_UNROLL + k]
                src = uv_hbm.at[pl.ds(pl.multiple_of(e * PEER_SLAB, PEER_SLAB), PEER_SLAB), :]
                dst = buf.at[slot, pl.ds((j * PEER_ISSUE_UNROLL + k) * PEER_PITCH, PEER_SLAB), :]
                pltpu.make_async_copy(src, dst, sems.at[slot]).start()
            return carry
        lax.fori_loop(0, nexp // PEER_ISSUE_UNROLL, group, 0)

    def wait(slot):
        span = pl.ds(0, nexp * PEER_SLAB)
        pltpu.make_async_copy(uv_hbm.at[span, :], buf.at[slot, span, :], sems.at[slot]).wait()

    def rows_of(slot, t, first_row):
        base = t * PEER_SLOTS * PEER_PITCH + first_row
        return jnp.concatenate(
            [jnp.concatenate([buf[slot, pl.ds(base + kg * 8 * PEER_PITCH + s, 8, stride=PEER_PITCH), :]
                              for s in range(D_MODEL // LANES)], axis=1)
             for kg in range(PEER_SLOTS // 8)], axis=0)

    tok_lane = lax.broadcasted_iota(jnp.int32, (PEER_SLOTS, PEER_TOKENS_PER_STEP), 1)
    gather(0, 0)

    def chunk(c, carry):
        slot = c % 2

        @pl.when(c + 1 < nchunk)
        def _():
            gather(c + 1, 1 - slot)

        wait(slot)
        for t in range(PEER_CHUNK):
            tok = c * PEER_CHUNK + t
            x = hn_ref[pl.ds(tok, 1), :]
            a = jax.nn.gelu(jnp.sum(rows_of(slot, t, 0) * x, -1, keepdims=True))
            gate = jnp.sum(jnp.where(tok_lane == tok, gt_ref[...], 0.0), -1, keepdims=True)
            out = jnp.sum((gate * a) * rows_of(slot, t, PEER_SLAB // 2), 0, keepdims=True)
            y_ref[pl.ds(tok, 1), :] = _rms(h_ref[pl.ds(tok, 1), :] + out, nf_ref[...])
        return carry

    lax.fori_loop(0, nchunk, chunk, 0)


def _expert_slabs(peer_u, peer_v):
    e = peer_u.shape[0]
    tiles = lambda a: a.reshape(e, D_MODEL // LANES, LANES)
    return jnp.concatenate([tiles(peer_u), tiles(peer_v)], axis=1).reshape(e * PEER_SLAB, LANES)


def _peer(eid_t, g_t, hn, h, nf_g, uv):
    n = hn.shape[0]
    tp = PEER_TOKENS_PER_STEP
    eid_tok = eid_t.T.reshape(n // tp, tp * PEER_SLOTS)
    rows = PEER_CHUNK * PEER_SLOTS * PEER_PITCH
    tok = pl.BlockSpec((tp, D_MODEL), lambda i: (i, 0))
    return pl.pallas_call(
        _peer_kernel,
        grid=(n // tp,),
        in_specs=[pl.BlockSpec(memory_space=pl.ANY),
                  pl.BlockSpec((PEER_SLOTS, tp), lambda i: (0, i)),
                  tok, tok,
                  pl.BlockSpec((1, D_MODEL), lambda i: (0, 0)),
                  pl.BlockSpec(memory_space=pl.ANY)],
        out_specs=tok,
        out_shape=jax.ShapeDtypeStruct((n, D_MODEL), F32),
        scratch_shapes=[pltpu.SMEM((tp * PEER_SLOTS,), jnp.int32),
                        pltpu.VMEM((2, rows, LANES), F32),
                        pltpu.SemaphoreType.DMA((3,))],
        compiler_params=_params("arbitrary"),
    )(eid_tok, g_t, hn, h, nf_g.reshape(1, D_MODEL), uv)


def _alibi_slopes():
    return (2.0 ** (-8.0 * (np.arange(ATTN_HEADS) + 1) / ATTN_HEADS)).astype(np.float32)


def _ret_consts():
    log_gamma = np.log(1.0 - 2.0 ** (-5.0 - np.arange(RET_HEADS))).astype(np.float32)
    pos = np.arange(RET_CHUNK, dtype=np.float32)
    rel = pos[:, None] - pos[None, :]
    dmat = np.where(rel[None] >= 0, np.exp(np.maximum(rel, 0.0)[None] * log_gamma[:, None, None]), 0.0)
    lanes = lambda col: np.broadcast_to(col[:, :, None], (RET_HEADS, col.shape[1], RET_HEAD_DIM))
    qdec = lanes(np.exp((pos[None, :] + 1.0) * log_gamma[:, None]))
    kdec = lanes(np.exp((RET_CHUNK - 1.0 - pos)[None, :] * log_gamma[:, None]))
    cdec = lanes(np.broadcast_to(np.exp(RET_CHUNK * log_gamma)[:, None], (RET_HEADS, 8)))
    gam = lanes(np.broadcast_to(np.exp(log_gamma)[:, None], (RET_HEADS, 8)))
    as_f32 = lambda a: jnp.asarray(np.ascontiguousarray(a, dtype=np.float32))
    return tuple(as_f32(a) for a in (dmat, qdec, kdec, cdec)), as_f32(gam)


def _channel_and_norm(attn, ret_o, gate, x, gn_g, w_out, n2_g, w_pq, keys, uv, nf_g):
    h, hn, st = _mixer_out(attn, ret_o, gate, x, gn_g, w_out, n2_g, w_pq, keys)
    eid_t, g_t = _topk(st)
    return _peer(eid_t, g_t, hn, h, nf_g, uv)


def kernel(x_prompt, x_sample, cache_k_win, cache_v_win, state_ret, norm1_g, w_in, ret_gn_g, w_out, norm2_g, w_pq, peer_sub_keys, peer_u, peer_v, norm_f_g):
    depth = w_in.shape[0]
    assert depth == 1, "single-layer stack"
    b, s, _ = x_prompt.shape
    bs = x_sample.shape[0]
    assert x_sample.shape[1] == 1 and s % (ATTN_BLOCK * max(DILATIONS)) == 0
    assert cache_k_win.shape[2] % (N_STEPS * max(DILATIONS)) == 0

    slopes = _alibi_slopes()
    slopes_tile = jnp.asarray(np.broadcast_to(slopes.reshape(ATTN_HEADS // 2, 2, 1, 1), (ATTN_HEADS // 2, 2, 8, LANES)).copy())
    slopes_heads = jnp.asarray(np.broadcast_to(slopes[:, None], (ATTN_HEADS, ATTN_HEAD_DIM)).copy())
    ret_consts, gam_tile = _ret_consts()

    w_in_b = w_in[0].astype(MXU_DTYPE)
    w_out_b = w_out[0].astype(MXU_DTYPE)
    w_pq_b = w_pq[0].astype(MXU_DTYPE)
    keys_b = peer_sub_keys[0].reshape(2 * PEER_HEADS, PEER_N_KEYS, PEER_HALF).astype(MXU_DTYPE)
    uv = _expert_slabs(peer_u[0], peer_v[0])
    tail = (ret_gn_g[0], w_out_b, norm2_g[0], w_pq_b, keys_b, uv, norm_f_g)

    xp = x_prompt.reshape(b * s, D_MODEL)
    aq, ak, av, rq, rk, rv, rg = _inproj(xp, norm1_g[0], w_in_b)
    seq = lambda a: a.reshape(b, s, ATTN_WIDTH)
    attn = _prompt_attention(seq(aq), seq(ak), seq(av), slopes_tile)
    ret_o, s_fin = _ret_prompt(seq(rq), seq(rk), seq(rv), ret_consts)
    y_prompt = _channel_and_norm(attn.reshape(b * s, ATTN_WIDTH), ret_o.reshape(b * s, RET_WIDTH), rg, xp, *tail)
    keep = min(N_STEPS * max(DILATIONS), s)
    k_win = ak.reshape(b, s, ATTN_HEADS, ATTN_HEAD_DIM)[:, s - keep:]
    v_win = av.reshape(b, s, ATTN_HEADS, ATTN_HEAD_DIM)[:, s - keep:]

    xs = x_sample.reshape(bs, D_MODEL)
    aq, ak, av, rq, rk, rv, rg = _inproj(xs, norm1_g[0], w_in_b)
    attn = _sample_attention(aq, ak, av, cache_k_win[0], cache_v_win[0], slopes_heads)
    ret_o, s_new = _ret_sample(rq, rk, rv, state_ret[0], gam_tile)
    y_sample = _channel_and_norm(attn, ret_o, rg, xs, *tail)

    return (y_prompt.reshape(b, s, D_MODEL), y_sample.reshape(bs, 1, D_MODEL),
            k_win[None], v_win[None], s_fin[None],
            ak.reshape(1, bs, 1, ATTN_HEADS, ATTN_HEAD_DIM), av.reshape(1, bs, 1, ATTN_HEADS, ATTN_HEAD_DIM),
            s_new[None])
```

```python
import functools

import numpy as np
import jax
import jax.numpy as jnp
from jax import lax
from jax.experimental import pallas as pl
from jax.experimental.pallas import tpu as pltpu

F32 = jnp.float32
MXU_DTYPE = jnp.bfloat16

D_MODEL = 1024
ATTN_WIDTH = 512
RET_WIDTH = 512
ATTN_HEAD_DIM = 64
ATTN_HEADS = 8
RET_HEADS = 4
RET_HEAD_DIM = 128
DILATIONS = (1, 4, 16)
N_STEPS = 128
ATTN_BLOCK = 128
ATTN_PAD = N_STEPS * max(DILATIONS)
RET_CHUNK = 128
PEER_HEADS = 8
PEER_N_KEYS = 128
PEER_TOPK = 16
PEER_HALF = 128
PEER_SLOTS = PEER_HEADS * PEER_TOPK
NORM_EPS = 1e-6
GN_EPS = 1e-5
IN_SPLITS = 7
RET_K_SCALE = RET_HEAD_DIM ** -0.5
ATTN_SCALE = ATTN_HEAD_DIM ** -0.5
LANES = 128
VMEM_LIMIT = 48 * 1024 * 1024

PEER_TOKENS_PER_STEP = 128
PEER_CHUNK = 8
PEER_SLAB = 2 * D_MODEL // LANES
PEER_PITCH = PEER_SLAB + 1
PEER_ISSUE_UNROLL = 8


def _params(*sem):
    return pltpu.CompilerParams(dimension_semantics=sem, vmem_limit_bytes=VMEM_LIMIT)


def _rms(x, g):
    r = lax.rsqrt(jnp.mean(x * x, axis=-1, keepdims=True) + NORM_EPS)
    return x * r * g


def _inproj_kernel(x_ref, g_ref, w_ref, *out_refs):
    tm = x_ref.shape[0]
    xn = _rms(x_ref[...], g_ref[...]).astype(MXU_DTYPE)
    for i, o_ref in enumerate(out_refs[:IN_SPLITS]):
        y = jnp.dot(xn, w_ref[:, i * ATTN_WIDTH:(i + 1) * ATTN_WIDTH], preferred_element_type=F32)
        if i == 4:
            y = y * RET_K_SCALE
        o_ref[...] = y
        if i in (1, 2):
            by_head = out_refs[IN_SPLITS + i - 1]
            for hd in range(ATTN_HEADS):
                by_head[pl.ds(hd, tm, stride=ATTN_HEADS), :] = y[:, hd * ATTN_HEAD_DIM:(hd + 1) * ATTN_HEAD_DIM]


def _inproj(x, g, w):
    n = x.shape[0]
    tm = min(256, n)
    out = jax.ShapeDtypeStruct((n, ATTN_WIDTH), F32)
    by_head = jax.ShapeDtypeStruct((n * ATTN_HEADS, ATTN_HEAD_DIM), F32)
    return pl.pallas_call(
        _inproj_kernel,
        grid=(n // tm,),
        in_specs=[pl.BlockSpec((tm, D_MODEL), lambda i: (i, 0)),
                  pl.BlockSpec((1, D_MODEL), lambda i: (0, 0)),
                  pl.BlockSpec((D_MODEL, IN_SPLITS * ATTN_WIDTH), lambda i: (0, 0))],
        out_specs=([pl.BlockSpec((tm, ATTN_WIDTH), lambda i: (i, 0))] * IN_SPLITS
                   + [pl.BlockSpec((tm * ATTN_HEADS, ATTN_HEAD_DIM), lambda i: (i, 0))] * 2),
        out_shape=[out] * IN_SPLITS + [by_head] * 2,
        compiler_params=_params("parallel"),
    )(x, g.reshape(1, D_MODEL), w)


def _merge_groups(outs, lses):
    m = functools.reduce(jnp.maximum, lses)
    es = [jnp.exp(l - m) for l in lses]
    den = functools.reduce(jnp.add, es)
    return functools.reduce(jnp.add, [(e / den) * o for e, o in zip(es, outs)])


def _prompt_attn_kernel(s_len, q_ref, k_ref, v_ref, slope_ref, o_ref, og_ref, lg_ref):
    lane = lax.broadcasted_iota(jnp.int32, (ATTN_BLOCK, LANES), 1)
    first = lane < ATTN_HEAD_DIM
    qi = lax.broadcasted_iota(jnp.int32, (ATTN_BLOCK, 2 * ATTN_BLOCK), 0)
    kj = lax.broadcasted_iota(jnp.int32, (ATTN_BLOCK, 2 * ATTN_BLOCK), 1)
    back = qi + ATTN_BLOCK - kj
    in_band = (back >= 0) & (back <= N_STEPS)
    nt = (((1,), (1,)), ((), ()))
    ntile = s_len // ATTN_BLOCK
    for g, dil in enumerate(DILATIONS):
        dist = (back * dil).astype(F32)
        bias = [jnp.where(in_band, -slope_ref[hh, 0:1, 0:1] * dist, -jnp.inf) for hh in range(2)]

        def tile(t, carry, g=g, dil=dil, bias=bias):
            r, n = t % dil, t // dil
            qrow = r + n * (dil * ATTN_BLOCK)
            krow = ATTN_PAD + qrow - dil * ATTN_BLOCK
            q = q_ref[pl.ds(qrow, ATTN_BLOCK, stride=dil), :]
            kb = k_ref[pl.ds(krow, 2 * ATTN_BLOCK, stride=dil), :].astype(MXU_DTYPE)
            vb = v_ref[pl.ds(krow, 2 * ATTN_BLOCK, stride=dil), :].astype(MXU_DTYPE)
            exists = kj + n * ATTN_BLOCK >= ATTN_BLOCK
            outs, lses = [], []
            for hh in range(2):
                qm = jnp.where(first if hh == 0 else ~first, q, 0.0).astype(MXU_DTYPE)
                sc = lax.dot_general(qm, kb, nt, preferred_element_type=F32) * ATTN_SCALE + bias[hh]
                sc = jnp.where(exists, sc, -jnp.inf)
                m = jnp.max(sc, -1, keepdims=True)
                p = jnp.exp(sc - m)
                den = jnp.sum(p, -1, keepdims=True)
                outs.append(jnp.dot((p / den).astype(MXU_DTYPE), vb, preferred_element_type=F32))
                lses.append(jnp.broadcast_to(m + jnp.log(den), (ATTN_BLOCK, LANES)))
            og_ref[g, pl.ds(qrow, ATTN_BLOCK, stride=dil), :] = jnp.where(first, outs[0], outs[1])
            lg_ref[g, pl.ds(qrow, ATTN_BLOCK, stride=dil), :] = jnp.where(first, lses[0], lses[1])
            return carry

        lax.fori_loop(0, ntile, tile, 0, unroll=2)

    def merge(i, carry):
        rows = pl.ds(pl.multiple_of(i * ATTN_BLOCK, ATTN_BLOCK), ATTN_BLOCK)
        groups = range(len(DILATIONS))
        o_ref[rows, :] = _merge_groups([og_ref[g, rows, :] for g in groups], [lg_ref[g, rows, :] for g in groups])
        return carry

    lax.fori_loop(0, ntile, merge, 0)


def _prompt_attention(q, k, v, slopes_tile):
    b, s, _ = q.shape
    pad = lambda a: jnp.pad(a, ((0, 0), (ATTN_PAD, 0), (0, 0)))
    q_spec = pl.BlockSpec((None, s, LANES), lambda i, h: (i, 0, h))
    kv_spec = pl.BlockSpec((None, ATTN_PAD + s, LANES), lambda i, h: (i, 0, h))
    return pl.pallas_call(
        functools.partial(_prompt_attn_kernel, s),
        grid=(b, ATTN_WIDTH // LANES),
        in_specs=[q_spec, kv_spec, kv_spec, pl.BlockSpec((None, 2, 8, LANES), lambda i, h: (h, 0, 0, 0))],
        out_specs=q_spec,
        out_shape=jax.ShapeDtypeStruct((b, s, ATTN_WIDTH), F32),
        scratch_shapes=[pltpu.VMEM((len(DILATIONS), s, LANES), F32), pltpu.VMEM((len(DILATIONS), s, LANES), F32)],
        compiler_params=_params("parallel", "parallel"),
    )(q, pad(k), pad(v), slopes_tile)


def _sample_attn_kernel(q_ref, kn_ref, vn_ref, slope_ref, *refs):
    kv_refs, o_ref = refs[:6], refs[6]
    q = q_ref[...]
    slope = slope_ref[:, 0:1]
    steps_back = (N_STEPS - lax.broadcasted_iota(jnp.int32, (N_STEPS, ATTN_HEADS, 1), 0)).astype(F32)
    l0 = jnp.sum(q * kn_ref[...], -1, keepdims=True) * ATTN_SCALE
    outs, lses = [], []
    for g, dil in enumerate(DILATIONS):
        kc = kv_refs[2 * g][...]
        vc = kv_refs[2 * g + 1][...]
        sc = jnp.sum(kc * q, -1, keepdims=True) * ATTN_SCALE - slope * (steps_back * float(dil))
        m = jnp.maximum(jnp.max(sc, 0), l0)
        p = jnp.exp(sc - m)
        p0 = jnp.exp(l0 - m)
        s = jnp.sum(p, 0) + p0
        outs.append((jnp.sum(p * vc, 0) + p0 * vn_ref[...]) / s)
        lses.append(m + jnp.log(s))
    o_ref[...] = _merge_groups(outs, lses)


def _sample_attention(q, kn, vn, cache_k, cache_v, slopes_heads):
    b, w = cache_k.shape[0], cache_k.shape[1]
    heads = lambda a: a.reshape(b, ATTN_HEADS, ATTN_HEAD_DIM)
    head_spec = pl.BlockSpec((None, ATTN_HEADS, ATTN_HEAD_DIM), lambda i: (i, 0, 0))
    kv, kv_specs = [], []
    for dil in DILATIONS:
        last = w // dil // N_STEPS - 1
        for c in (cache_k, cache_v):
            kv.append(c.reshape(b, w // dil, dil, ATTN_HEADS, ATTN_HEAD_DIM))
            kv_specs.append(pl.BlockSpec((None, N_STEPS, None, ATTN_HEADS, ATTN_HEAD_DIM),
                                         lambda i, last=last: (i, last, 0, 0, 0)))
    out = pl.pallas_call(
        _sample_attn_kernel,
        grid=(b,),
        in_specs=[head_spec, head_spec, head_spec,
                  pl.BlockSpec((ATTN_HEADS, ATTN_HEAD_DIM), lambda i: (0, 0))] + kv_specs,
        out_specs=head_spec,
        out_shape=jax.ShapeDtypeStruct((b, ATTN_HEADS, ATTN_HEAD_DIM), F32),
        compiler_params=_params("parallel"),
    )(heads(q), heads(kn), heads(vn), slopes_heads, *kv)
    return out.reshape(b, ATTN_WIDTH)


def _ret_prompt_kernel(nchunk, q_ref, k_ref, v_ref, dmat_ref, qdec_ref, kdec_ref, cdec_ref, o_ref, s_ref):
    s_ref[...] = jnp.zeros_like(s_ref)
    nt = (((1,), (1,)), ((), ()))
    tn = (((0,), (0,)), ((), ()))

    def chunk(c, carry):
        row = pl.multiple_of(c * RET_CHUNK, RET_CHUNK)
        q = q_ref[pl.ds(row, RET_CHUNK), :]
        k = k_ref[pl.ds(row, RET_CHUNK), :]
        v = v_ref[pl.ds(row, RET_CHUNK), :].astype(MXU_DTYPE)
        qb = q.astype(MXU_DTYPE)
        state = s_ref[...]
        inner = lax.dot_general(qb, k.astype(MXU_DTYPE), nt, preferred_element_type=F32) * dmat_ref[...]
        o_in = jnp.dot(inner.astype(MXU_DTYPE), v, preferred_element_type=F32)
        o_x = jnp.dot(qb, state.astype(MXU_DTYPE), preferred_element_type=F32) * qdec_ref[...]
        o_ref[pl.ds(row, RET_CHUNK), :] = o_in + o_x
        kd = (k * kdec_ref[...]).astype(MXU_DTYPE)
        s_ref[...] = cdec_ref[0:1, :] * state + lax.dot_general(kd, v, tn, preferred_element_type=F32)
        return carry

    lax.fori_loop(0, nchunk, chunk, 0)


def _ret_prompt(q, k, v, consts):
    b, s, _ = q.shape
    seq = pl.BlockSpec((None, s, RET_HEAD_DIM), lambda i, h: (i, 0, h))
    per_head = lambda rows: pl.BlockSpec((None, rows, RET_HEAD_DIM), lambda i, h: (h, 0, 0))
    return pl.pallas_call(
        functools.partial(_ret_prompt_kernel, s // RET_CHUNK),
        grid=(b, RET_HEADS),
        in_specs=[seq, seq, seq, per_head(RET_CHUNK), per_head(RET_CHUNK), per_head(RET_CHUNK), per_head(8)],
        out_specs=[seq, pl.BlockSpec((None, None, RET_HEAD_DIM, RET_HEAD_DIM), lambda i, h: (i, h, 0, 0))],
        out_shape=[jax.ShapeDtypeStruct((b, s, RET_WIDTH), F32),
                   jax.ShapeDtypeStruct((b, RET_HEADS, RET_HEAD_DIM, RET_HEAD_DIM), F32)],
        compiler_params=_params("parallel", "parallel"),
    )(q, k, v, *consts)


def _ret_sample_kernel(qc_ref, kc_ref, v_ref, s_ref, gam_ref, o_ref, sn_ref):
    for h in range(RET_HEADS):
        qc = qc_ref[h]
        kc = kc_ref[h]
        v = v_ref[:, h * RET_HEAD_DIM:(h + 1) * RET_HEAD_DIM]
        gam = gam_ref[h, 0:1, :]
        state = s_ref[h]
        inner = jnp.sum(qc * kc, 0, keepdims=True)
        o_x = jnp.sum(qc * state, 0, keepdims=True)
        o_ref[:, h * RET_HEAD_DIM:(h + 1) * RET_HEAD_DIM] = inner * v + gam * o_x
        sn_ref[h] = gam * state + kc * v


def _ret_sample(q, k, v, state, gam_tile):
    b = q.shape[0]
    col = lambda a: a.reshape(b, RET_HEADS, RET_HEAD_DIM, 1)
    col_spec = pl.BlockSpec((None, RET_HEADS, RET_HEAD_DIM, 1), lambda i: (i, 0, 0, 0))
    row_spec = pl.BlockSpec((None, 1, RET_WIDTH), lambda i: (i, 0, 0))
    st_spec = pl.BlockSpec((None, RET_HEADS, RET_HEAD_DIM, RET_HEAD_DIM), lambda i: (i, 0, 0, 0))
    o, sn = pl.pallas_call(
        _ret_sample_kernel,
        grid=(b,),
        in_specs=[col_spec, col_spec, row_spec, st_spec,
                  pl.BlockSpec((RET_HEADS, 8, RET_HEAD_DIM), lambda i: (0, 0, 0))],
        out_specs=[row_spec, st_spec],
        out_shape=[jax.ShapeDtypeStruct((b, 1, RET_WIDTH), F32), jax.ShapeDtypeStruct(state.shape, F32)],
        compiler_params=_params("parallel"),
    )(col(q), col(k), v.reshape(b, 1, RET_WIDTH), state, gam_tile)
    return o.reshape(b, RET_WIDTH), sn


def _mixer_out_kernel(attn_ref, ret_ref, gate_ref, x_ref, gn_ref, wout_ref, n2_ref, wpq_ref, keys_ref,
                      h_ref, hn_ref, st_ref):
    gate = gate_ref[...]
    swish = gate * (1.0 / (1.0 + jnp.exp(-gate)))
    normed = []
    for hd in range(RET_HEADS):
        cols = slice(hd * RET_HEAD_DIM, (hd + 1) * RET_HEAD_DIM)
        r = ret_ref[:, cols]
        mu = jnp.mean(r, -1, keepdims=True)
        var = jnp.mean(jnp.square(r - mu), -1, keepdims=True)
        normed.append((r - mu) * lax.rsqrt(var + GN_EPS) * gn_ref[:, cols])
    ret_y = swish * jnp.concatenate(normed, -1)
    h = (x_ref[...]
         + jnp.dot(attn_ref[...].astype(MXU_DTYPE), wout_ref[0:ATTN_WIDTH, :], preferred_element_type=F32)
         + jnp.dot(ret_y.astype(MXU_DTYPE), wout_ref[ATTN_WIDTH:, :], preferred_element_type=F32))
    h_ref[...] = h
    hn = _rms(h, n2_ref[...])
    hn_ref[...] = hn
    hb = hn.astype(MXU_DTYPE)
    nt = (((1,), (1,)), ((), ()))
    for j in range(2 * PEER_HEADS):
        qj = jnp.dot(hb, wpq_ref[:, j * PEER_HALF:(j + 1) * PEER_HALF], preferred_element_type=F32)
        st_ref[j] = lax.dot_general(keys_ref[j], qj.astype(MXU_DTYPE), nt, preferred_element_type=F32)


def _mixer_out(attn, ret_o, gate, x, gn_g, w_out, n2_g, w_pq, keys):
    n = x.shape[0]
    tm = min(256, n)
    half = pl.BlockSpec((tm, ATTN_WIDTH), lambda i: (i, 0))
    full = pl.BlockSpec((tm, D_MODEL), lambda i: (i, 0))
    const = lambda shape: pl.BlockSpec(shape, lambda i: (0,) * len(shape))
    return pl.pallas_call(
        _mixer_out_kernel,
        grid=(n // tm,),
        in_specs=[half] * 3 + [full, const((1, RET_WIDTH)), const((D_MODEL, D_MODEL)), const((1, D_MODEL)),
                               const((D_MODEL, 2 * PEER_HEADS * PEER_HALF)),
                               const((2 * PEER_HEADS, PEER_N_KEYS, PEER_HALF))],
        out_specs=[full, full, pl.BlockSpec((2 * PEER_HEADS, PEER_N_KEYS, tm), lambda i: (0, 0, i))],
        out_shape=[jax.ShapeDtypeStruct((n, D_MODEL), F32), jax.ShapeDtypeStruct((n, D_MODEL), F32),
                   jax.ShapeDtypeStruct((2 * PEER_HEADS, PEER_N_KEYS, n), F32)],
        compiler_params=_params("parallel"),
    )(attn, ret_o, gate, x, gn_g.reshape(1, RET_WIDTH), w_out, n2_g.reshape(1, D_MODEL), w_pq, keys)


def _top_rows(s, k, payload=None):
    nrow = s.shape[0]
    ridx = lax.broadcasted_iota(jnp.int32, s.shape, 0)
    vals, picks = [], []
    for _ in range(k):
        m = jnp.max(s, 0, keepdims=True)
        am = jnp.min(jnp.where(s == m, ridx, nrow), 0, keepdims=True)
        hit = ridx == am
        vals.append(m)
        picks.append(am if payload is None else jnp.sum(jnp.where(hit, payload, 0), 0, keepdims=True))
        s = jnp.where(hit, -jnp.inf, s)
    return jnp.concatenate(vals, 0), jnp.concatenate(picks, 0)


def _product_grid(a0, a1, op):
    pieces = [op(a0[0:1], a1)]
    pieces += [op(a0[a:a + 1], a1[0:8]) for a in range(1, 8)]
    pieces.append(op(a0[8:16], a1[0:1]))
    return jnp.concatenate(pieces, 0)


def _topk_kernel(st_ref, eid_ref, g_ref):
    def head(hd, carry):
        sv0, si0 = _top_rows(st_ref[2 * hd], PEER_TOPK)
        sv1, si1 = _top_rows(st_ref[2 * hd + 1], PEER_TOPK)
        cand = _product_grid(sv0, sv1, jnp.add)
        cidx = _product_grid(si0, si1, lambda i, j: i * PEER_N_KEYS + j)
        best, eid = _top_rows(cand, PEER_TOPK, payload=cidx)
        e = jnp.exp(best - jnp.max(best, 0, keepdims=True))
        rows = pl.ds(pl.multiple_of(hd * PEER_TOPK, PEER_TOPK), PEER_TOPK)
        eid_ref[rows, :] = eid
        g_ref[rows, :] = e / jnp.sum(e, 0, keepdims=True)
        return carry

    lax.fori_loop(0, PEER_HEADS, head, 0)


def _topk(st):
    n = st.shape[-1]
    tn = LANES
    out_spec = pl.BlockSpec((PEER_SLOTS, tn), lambda i: (0, i))
    return pl.pallas_call(
        _topk_kernel,
        grid=(n // tn,),
        in_specs=[pl.BlockSpec((2 * PEER_HEADS, PEER_N_KEYS, tn), lambda i: (0, 0, i))],
        out_specs=[out_spec, out_spec],
        out_shape=[jax.ShapeDtypeStruct((PEER_SLOTS, n), jnp.int32), jax.ShapeDtypeStruct((PEER_SLOTS, n), F32)],
        compiler_params=_params("parallel"),
    )(st)


def _peer_kernel(eid_hbm, gt_ref, hn_ref, h_ref, nf_ref, uv_hbm, y_ref, eid_smem, buf, sems):
    step = pl.program_id(0)
    nchunk = PEER_TOKENS_PER_STEP // PEER_CHUNK
    nexp = PEER_CHUNK * PEER_SLOTS
    ids = pltpu.make_async_copy(eid_hbm.at[step], eid_smem, sems.at[2])
    ids.start()
    ids.wait()

    def gather(c, slot):
        def group(j, carry):
            for k in range(PEER_ISSUE_UNROLL):
                e = eid_smem[c * nexp + j * PEER_ISSUE_UNROLL + k]
                src = uv_hbm.at[pl.ds(pl.multiple_of(e * PEER_SLAB, PEER_SLAB), PEER_SLAB), :]
                dst = buf.at[slot, pl.ds((j * PEER_ISSUE_UNROLL + k) * PEER_PITCH, PEER_SLAB), :]
                pltpu.make_async_copy(src, dst, sems.at[slot]).start(priority=k % 2)
            return carry
        lax.fori_loop(0, nexp // PEER_ISSUE_UNROLL, group, 0)

    def wait(slot):
        span = pl.ds(0, nexp * PEER_SLAB)
        pltpu.make_async_copy(uv_hbm.at[span, :], buf.at[slot, span, :], sems.at[slot]).wait()

    def rows_of(slot, t, first_row):
        base = t * PEER_SLOTS * PEER_PITCH + first_row
        return jnp.concatenate(
            [jnp.concatenate([buf[slot, pl.ds(base + kg * 8 * PEER_PITCH + s, 8, stride=PEER_PITCH), :]
                              for s in range(D_MODEL // LANES)], axis=1)
             for kg in range(PEER_SLOTS // 8)], axis=0)

    tok_lane = lax.broadcasted_iota(jnp.int32, (PEER_SLOTS, PEER_TOKENS_PER_STEP), 1)
    gather(0, 0)

    def chunk(c, carry):
        slot = c % 2

        @pl.when(c + 1 < nchunk)
        def _():
            gather(c + 1, 1 - slot)

        wait(slot)
        for t in range(PEER_CHUNK):
            tok = c * PEER_CHUNK + t
            x = hn_ref[pl.ds(tok, 1), :]
            a = jax.nn.gelu(jnp.sum(rows_of(slot, t, 0) * x, -1, keepdims=True))
            gate = jnp.sum(jnp.where(tok_lane == tok, gt_ref[...], 0.0), -1, keepdims=True)
            out = jnp.sum((gate * a) * rows_of(slot, t, PEER_SLAB // 2), 0, keepdims=True)
            y_ref[pl.ds(tok, 1), :] = _rms(h_ref[pl.ds(tok, 1), :] + out, nf_ref[...])
        return carry

    lax.fori_loop(0, nchunk, chunk, 0)


def _expert_slabs(peer_u, peer_v):
    e = peer_u.shape[0]
    tiles = lambda a: a.reshape(e, D_MODEL // LANES, LANES)
    return jnp.concatenate([tiles(peer_u), tiles(peer_v)], axis=1).reshape(e * PEER_SLAB, LANES)


def _peer(eid_t, g_t, hn, h, nf_g, uv):
    n = hn.shape[0]
    tp = PEER_TOKENS_PER_STEP
    eid_tok = eid_t.T.reshape(n // tp, tp * PEER_SLOTS)
    rows = PEER_CHUNK * PEER_SLOTS * PEER_PITCH
    tok = pl.BlockSpec((tp, D_MODEL), lambda i: (i, 0))
    return pl.pallas_call(
        _peer_kernel,
        grid=(n // tp,),
        in_specs=[pl.BlockSpec(memory_space=pl.ANY),
                  pl.BlockSpec((PEER_SLOTS, tp), lambda i: (0, i)),
                  tok, tok,
                  pl.BlockSpec((1, D_MODEL), lambda i: (0, 0)),
                  pl.BlockSpec(memory_space=pl.ANY)],
        out_specs=tok,
        out_shape=jax.ShapeDtypeStruct((n, D_MODEL), F32),
        scratch_shapes=[pltpu.SMEM((tp * PEER_SLOTS,), jnp.int32),
                        pltpu.VMEM((2, rows, LANES), F32),
                        pltpu.SemaphoreType.DMA((3,))],
        compiler_params=_params("arbitrary"),
    )(eid_tok, g_t, hn, h, nf_g.reshape(1, D_MODEL), uv)


def _alibi_slopes():
    return (2.0 ** (-8.0 * (np.arange(ATTN_HEADS) + 1) / ATTN_HEADS)).astype(np.float32)


def _ret_consts():
    log_gamma = np.log(1.0 - 2.0 ** (-5.0 - np.arange(RET_HEADS))).astype(np.float32)
    pos = np.arange(RET_CHUNK, dtype=np.float32)
    rel = pos[:, None] - pos[None, :]
    dmat = np.where(rel[None] >= 0, np.exp(np.maximum(rel, 0.0)[None] * log_gamma[:, None, None]), 0.0)
    lanes = lambda col: np.broadcast_to(col[:, :, None], (RET_HEADS, col.shape[1], RET_HEAD_DIM))
    qdec = lanes(np.exp((pos[None, :] + 1.0) * log_gamma[:, None]))
    kdec = lanes(np.exp((RET_CHUNK - 1.0 - pos)[None, :] * log_gamma[:, None]))
    cdec = lanes(np.broadcast_to(np.exp(RET_CHUNK * log_gamma)[:, None], (RET_HEADS, 8)))
    gam = lanes(np.broadcast_to(np.exp(log_gamma)[:, None], (RET_HEADS, 8)))
    as_f32 = lambda a: jnp.asarray(np.ascontiguousarray(a, dtype=np.float32))
    return tuple(as_f32(a) for a in (dmat, qdec, kdec, cdec)), as_f32(gam)


def _channel_and_norm(attn, ret_o, gate, x, gn_g, w_out, n2_g, w_pq, keys, uv, nf_g):
    h, hn, st = _mixer_out(attn, ret_o, gate, x, gn_g, w_out, n2_g, w_pq, keys)
    eid_t, g_t = _topk(st)
    return _peer(eid_t, g_t, hn, h, nf_g, uv)


def kernel(x_prompt, x_sample, cache_k_win, cache_v_win, state_ret, norm1_g, w_in, ret_gn_g, w_out, norm2_g, w_pq, peer_sub_keys, peer_u, peer_v, norm_f_g):
    depth = w_in.shape[0]
    assert depth == 1, "single-layer stack"
    b, s, _ = x_prompt.shape
    bs = x_sample.shape[0]
    assert x_sample.shape[1] == 1 and s % (ATTN_BLOCK * max(DILATIONS)) == 0
    assert cache_k_win.shape[2] % (N_STEPS * max(DILATIONS)) == 0

    slopes = _alibi_slopes()
    slopes_tile = jnp.asarray(np.broadcast_to(slopes.reshape(ATTN_HEADS // 2, 2, 1, 1), (ATTN_HEADS // 2, 2, 8, LANES)).copy())
    slopes_heads = jnp.asarray(np.broadcast_to(slopes[:, None], (ATTN_HEADS, ATTN_HEAD_DIM)).copy())
    ret_consts, gam_tile = _ret_consts()

    w_in_b = w_in[0].astype(MXU_DTYPE)
    w_out_b = w_out[0].astype(MXU_DTYPE)
    w_pq_b = w_pq[0].astype(MXU_DTYPE)
    keys_b = peer_sub_keys[0].reshape(2 * PEER_HEADS, PEER_N_KEYS, PEER_HALF).astype(MXU_DTYPE)
    uv = _expert_slabs(peer_u[0], peer_v[0])
    tail = (ret_gn_g[0], w_out_b, norm2_g[0], w_pq_b, keys_b, uv, norm_f_g)

    xp = x_prompt.reshape(b * s, D_MODEL)
    aq, ak, av, rq, rk, rv, rg, kh, vh = _inproj(xp, norm1_g[0], w_in_b)
    seq = lambda a: a.reshape(b, s, ATTN_WIDTH)
    attn = _prompt_attention(seq(aq), seq(ak), seq(av), slopes_tile)
    ret_o, s_fin = _ret_prompt(seq(rq), seq(rk), seq(rv), ret_consts)
    y_prompt = _channel_and_norm(attn.reshape(b * s, ATTN_WIDTH), ret_o.reshape(b * s, RET_WIDTH), rg, xp, *tail)
    keep = min(N_STEPS * max(DILATIONS), s)
    k_win = kh.reshape(b, s, ATTN_HEADS, ATTN_HEAD_DIM)[:, s - keep:]
    v_win = vh.reshape(b, s, ATTN_HEADS, ATTN_HEAD_DIM)[:, s - keep:]

    xs = x_sample.reshape(bs, D_MODEL)
    aq, _, _, rq, rk, rv, rg, kh, vh = _inproj(xs, norm1_g[0], w_in_b)
    k_new = kh.reshape(bs, ATTN_HEADS, ATTN_HEAD_DIM)
    v_new = vh.reshape(bs, ATTN_HEADS, ATTN_HEAD_DIM)
    attn = _sample_attention(aq, k_new, v_new, cache_k_win[0], cache_v_win[0], slopes_heads)
    ret_o, s_new = _ret_sample(rq, rk, rv, state_ret[0], gam_tile)
    y_sample = _channel_and_norm(attn, ret_o, rg, xs, *tail)

    return (y_prompt.reshape(b, s, D_MODEL), y_sample.reshape(bs, 1, D_MODEL),
            k_win[None], v_win[None], s_fin[None],
            k_new.reshape(1, bs, 1, ATTN_HEADS, ATTN_HEAD_DIM), v_new.reshape(1, bs, 1, ATTN_HEADS, ATTN_HEAD_DIM),
            s_new[None])
```

```python
import functools

import numpy as np
import jax
import jax.numpy as jnp
from jax import lax
from jax.experimental import pallas as pl
from jax.experimental.pallas import tpu as pltpu

F32 = jnp.float32
MXU_DTYPE = jnp.bfloat16

D_MODEL = 1024
ATTN_WIDTH = 512
RET_WIDTH = 512
ATTN_HEAD_DIM = 64
ATTN_HEADS = 8
RET_HEADS = 4
RET_HEAD_DIM = 128
DILATIONS = (1, 4, 16)
N_STEPS = 128
ATTN_BLOCK = 128
ATTN_PAD = N_STEPS * max(DILATIONS)
RET_CHUNK = 128
PEER_HEADS = 8
PEER_N_KEYS = 128
PEER_TOPK = 16
PEER_HALF = 128
PEER_SLOTS = PEER_HEADS * PEER_TOPK
NORM_EPS = 1e-6
GN_EPS = 1e-5
IN_SPLITS = 7
RET_K_SCALE = RET_HEAD_DIM ** -0.5
ATTN_SCALE = ATTN_HEAD_DIM ** -0.5
LANES = 128
VMEM_LIMIT = 48 * 1024 * 1024

PEER_TOKENS_PER_STEP = 128
PEER_CHUNK = 8
PEER_SLAB = 2 * D_MODEL // LANES
PEER_PITCH = PEER_SLAB + 1
PEER_ISSUE_UNROLL = 8


def _params(*sem):
    return pltpu.CompilerParams(dimension_semantics=sem, vmem_limit_bytes=VMEM_LIMIT)


def _rms(x, g):
    r = lax.rsqrt(jnp.mean(x * x, axis=-1, keepdims=True) + NORM_EPS)
    return x * r * g


def _inproj_kernel(x_ref, g_ref, w_ref, *refs):
    out_refs = refs[-IN_SPLITS:] if len(refs) == IN_SPLITS else refs[1:1 + IN_SPLITS]
    xn = _rms(x_ref[...], g_ref[...]).astype(MXU_DTYPE)
    for i, o_ref in enumerate(out_refs):
        y = jnp.dot(xn, w_ref[:, i * ATTN_WIDTH:(i + 1) * ATTN_WIDTH], preferred_element_type=F32)
        if i == 4:
            y = y * RET_K_SCALE
        o_ref[...] = y
    if len(refs) > IN_SPLITS:
        wkv_t_ref, kt_ref, vt_ref = refs[0], refs[-2], refs[-1]
        nt = (((1,), (1,)), ((), ()))
        kv_t = lax.dot_general(wkv_t_ref[...], xn, nt, preferred_element_type=F32)
        kt_ref[...] = kv_t[:ATTN_WIDTH]
        vt_ref[...] = kv_t[ATTN_WIDTH:]


def _inproj(x, g, w, seq_len=None):
    n = x.shape[0]
    tm = min(256, n)
    out = jax.ShapeDtypeStruct((n, ATTN_WIDTH), F32)
    in_specs = [pl.BlockSpec((tm, D_MODEL), lambda i: (i, 0)),
                pl.BlockSpec((1, D_MODEL), lambda i: (0, 0)),
                pl.BlockSpec((D_MODEL, IN_SPLITS * ATTN_WIDTH), lambda i: (0, 0))]
    out_specs = [pl.BlockSpec((tm, ATTN_WIDTH), lambda i: (i, 0))] * IN_SPLITS
    out_shape = [out] * IN_SPLITS
    args = [x, g.reshape(1, D_MODEL), w]
    if seq_len is not None:
        per_row = seq_len // tm
        in_specs.append(pl.BlockSpec((2 * ATTN_WIDTH, D_MODEL), lambda i: (0, 0)))
        args.append(w[:, ATTN_WIDTH:3 * ATTN_WIDTH].T)
        out_specs += [pl.BlockSpec((None, ATTN_WIDTH, tm), lambda i: (i // per_row, 0, i % per_row))] * 2
        out_shape += [jax.ShapeDtypeStruct((n // seq_len, ATTN_WIDTH, seq_len), F32)] * 2
    return pl.pallas_call(
        _inproj_kernel,
        grid=(n // tm,),
        in_specs=in_specs,
        out_specs=out_specs,
        out_shape=out_shape,
        compiler_params=_params("parallel"),
    )(*args)


def _merge_groups(outs, lses):
    m = functools.reduce(jnp.maximum, lses)
    es = [jnp.exp(l - m) for l in lses]
    den = functools.reduce(jnp.add, es)
    return functools.reduce(jnp.add, [(e / den) * o for e, o in zip(es, outs)])


def _prompt_attn_kernel(s_len, q_ref, k_ref, v_ref, slope_ref, o_ref, og_ref, lg_ref):
    lane = lax.broadcasted_iota(jnp.int32, (ATTN_BLOCK, LANES), 1)
    first = lane < ATTN_HEAD_DIM
    qi = lax.broadcasted_iota(jnp.int32, (ATTN_BLOCK, 2 * ATTN_BLOCK), 0)
    kj = lax.broadcasted_iota(jnp.int32, (ATTN_BLOCK, 2 * ATTN_BLOCK), 1)
    back = qi + ATTN_BLOCK - kj
    in_band = (back >= 0) & (back <= N_STEPS)
    nt = (((1,), (1,)), ((), ()))
    ntile = s_len // ATTN_BLOCK
    for g, dil in enumerate(DILATIONS):
        dist = (back * dil).astype(F32)
        bias = [jnp.where(in_band, -slope_ref[hh, 0:1, 0:1] * dist, -jnp.inf) for hh in range(2)]

        def tile(t, carry, g=g, dil=dil, bias=bias):
            r, n = t % dil, t // dil
            qrow = r + n * (dil * ATTN_BLOCK)
            krow = ATTN_PAD + qrow - dil * ATTN_BLOCK
            q = q_ref[pl.ds(qrow, ATTN_BLOCK, stride=dil), :]
            kb = k_ref[pl.ds(krow, 2 * ATTN_BLOCK, stride=dil), :].astype(MXU_DTYPE)
            vb = v_ref[pl.ds(krow, 2 * ATTN_BLOCK, stride=dil), :].astype(MXU_DTYPE)
            exists = kj + n * ATTN_BLOCK >= ATTN_BLOCK
            outs, lses = [], []
            for hh in range(2):
                qm = jnp.where(first if hh == 0 else ~first, q, 0.0).astype(MXU_DTYPE)
                sc = lax.dot_general(qm, kb, nt, preferred_element_type=F32) * ATTN_SCALE + bias[hh]
                sc = jnp.where(exists, sc, -jnp.inf)
                m = jnp.max(sc, -1, keepdims=True)
                p = jnp.exp(sc - m)
                den = jnp.sum(p, -1, keepdims=True)
                outs.append(jnp.dot((p / den).astype(MXU_DTYPE), vb, preferred_element_type=F32))
                lses.append(jnp.broadcast_to(m + jnp.log(den), (ATTN_BLOCK, LANES)))
            og_ref[g, pl.ds(qrow, ATTN_BLOCK, stride=dil), :] = jnp.where(first, outs[0], outs[1])
            lg_ref[g, pl.ds(qrow, ATTN_BLOCK, stride=dil), :] = jnp.where(first, lses[0], lses[1])
            return carry

        lax.fori_loop(0, ntile, tile, 0, unroll=2)

    def merge(i, carry):
        rows = pl.ds(pl.multiple_of(i * ATTN_BLOCK, ATTN_BLOCK), ATTN_BLOCK)
        groups = range(len(DILATIONS))
        o_ref[rows, :] = _merge_groups([og_ref[g, rows, :] for g in groups], [lg_ref[g, rows, :] for g in groups])
        return carry

    lax.fori_loop(0, ntile, merge, 0)


def _prompt_attention(q, k, v, slopes_tile):
    b, s, _ = q.shape
    pad = lambda a: jnp.pad(a, ((0, 0), (ATTN_PAD, 0), (0, 0)))
    q_spec = pl.BlockSpec((None, s, LANES), lambda i, h: (i, 0, h))
    kv_spec = pl.BlockSpec((None, ATTN_PAD + s, LANES), lambda i, h: (i, 0, h))
    return pl.pallas_call(
        functools.partial(_prompt_attn_kernel, s),
        grid=(b, ATTN_WIDTH // LANES),
        in_specs=[q_spec, kv_spec, kv_spec, pl.BlockSpec((None, 2, 8, LANES), lambda i, h: (h, 0, 0, 0))],
        out_specs=q_spec,
        out_shape=jax.ShapeDtypeStruct((b, s, ATTN_WIDTH), F32),
        scratch_shapes=[pltpu.VMEM((len(DILATIONS), s, LANES), F32), pltpu.VMEM((len(DILATIONS), s, LANES), F32)],
        compiler_params=_params("parallel", "parallel"),
    )(q, pad(k), pad(v), slopes_tile)


def _sample_attn_kernel(slopes, q_ref, kn_ref, vn_ref, kt_ref, vt_ref, o_ref):
    w = kt_ref.shape[-1]
    back_i = w - lax.broadcasted_iota(jnp.int32, (1, w), 1)
    back = back_i.astype(F32)
    valid = [(back_i % dil == 0) & (back_i <= N_STEPS * dil) for dil in DILATIONS]
    for hd in range(ATTN_HEADS):
        q = q_ref[hd]
        raw = jnp.sum(kt_ref[hd] * q, 0, keepdims=True) * ATTN_SCALE - slopes[hd] * back
        l0 = jnp.sum(q * kn_ref[hd], 0, keepdims=True) * ATTN_SCALE
        vt = vt_ref[hd]
        outs, lses = [], []
        for ok in valid:
            m = jnp.maximum(jnp.max(jnp.where(ok, raw, -jnp.inf), -1, keepdims=True), l0)
            p = jnp.where(ok, jnp.exp(raw - m), 0.0)
            p0 = jnp.exp(l0 - m)
            s = jnp.sum(p, -1, keepdims=True) + p0
            outs.append((jnp.sum(vt * p, -1, keepdims=True) + p0 * vn_ref[hd]) / s)
            lses.append(m + jnp.log(s))
        o_ref[hd] = _merge_groups(outs, lses)


def _sample_attention(q, kn, vn, cache_kt, cache_vt):
    b = q.shape[0]
    assert cache_kt.shape[-1] >= N_STEPS * max(DILATIONS)
    cols = lambda a: a.reshape(b, ATTN_HEADS, ATTN_HEAD_DIM, 1)
    col_spec = pl.BlockSpec((None, ATTN_HEADS, ATTN_HEAD_DIM, 1), lambda i: (i, 0, 0, 0))
    win_spec = pl.BlockSpec((None,) + cache_kt.shape[1:], lambda i: (i, 0, 0, 0))
    out = pl.pallas_call(
        functools.partial(_sample_attn_kernel, [float(v) for v in _alibi_slopes()]),
        grid=(b,),
        in_specs=[col_spec, col_spec, col_spec, win_spec, win_spec],
        out_specs=col_spec,
        out_shape=jax.ShapeDtypeStruct((b, ATTN_HEADS, ATTN_HEAD_DIM, 1), F32),
        compiler_params=_params("parallel"),
    )(cols(q), cols(kn), cols(vn), cache_kt, cache_vt)
    return out.reshape(b, ATTN_WIDTH)


def _ret_prompt_kernel(nchunk, q_ref, k_ref, v_ref, dmat_ref, qdec_ref, kdec_ref, cdec_ref, o_ref, s_ref):
    s_ref[...] = jnp.zeros_like(s_ref)
    nt = (((1,), (1,)), ((), ()))
    tn = (((0,), (0,)), ((), ()))

    def chunk(c, carry):
        row = pl.multiple_of(c * RET_CHUNK, RET_CHUNK)
        q = q_ref[pl.ds(row, RET_CHUNK), :]
        k = k_ref[pl.ds(row, RET_CHUNK), :]
        v = v_ref[pl.ds(row, RET_CHUNK), :].astype(MXU_DTYPE)
        qb = q.astype(MXU_DTYPE)
        state = s_ref[...]
        inner = lax.dot_general(qb, k.astype(MXU_DTYPE), nt, preferred_element_type=F32) * dmat_ref[...]
        o_in = jnp.dot(inner.astype(MXU_DTYPE), v, preferred_element_type=F32)
        o_x = jnp.dot(qb, state.astype(MXU_DTYPE), preferred_element_type=F32) * qdec_ref[...]
        o_ref[pl.ds(row, RET_CHUNK), :] = o_in + o_x
        kd = (k * kdec_ref[...]).astype(MXU_DTYPE)
        s_ref[...] = cdec_ref[0:1, :] * state + lax.dot_general(kd, v, tn, preferred_element_type=F32)
        return carry

    lax.fori_loop(0, nchunk, chunk, 0)


def _ret_prompt(q, k, v, consts):
    b, s, _ = q.shape
    seq = pl.BlockSpec((None, s, RET_HEAD_DIM), lambda i, h: (i, 0, h))
    per_head = lambda rows: pl.BlockSpec((None, rows, RET_HEAD_DIM), lambda i, h: (h, 0, 0))
    return pl.pallas_call(
        functools.partial(_ret_prompt_kernel, s // RET_CHUNK),
        grid=(b, RET_HEADS),
        in_specs=[seq, seq, seq, per_head(RET_CHUNK), per_head(RET_CHUNK), per_head(RET_CHUNK), per_head(8)],
        out_specs=[seq, pl.BlockSpec((None, None, RET_HEAD_DIM, RET_HEAD_DIM), lambda i, h: (i, h, 0, 0))],
        out_shape=[jax.ShapeDtypeStruct((b, s, RET_WIDTH), F32),
                   jax.ShapeDtypeStruct((b, RET_HEADS, RET_HEAD_DIM, RET_HEAD_DIM), F32)],
        compiler_params=_params("parallel", "parallel"),
    )(q, k, v, *consts)


def _ret_sample_kernel(qc_ref, kc_ref, v_ref, s_ref, gam_ref, o_ref, sn_ref):
    for h in range(RET_HEADS):
        qc = qc_ref[h]
        kc = kc_ref[h]
        v = v_ref[:, h * RET_HEAD_DIM:(h + 1) * RET_HEAD_DIM]
        gam = gam_ref[h, 0:1, :]
        state = s_ref[h]
        inner = jnp.sum(qc * kc, 0, keepdims=True)
        o_x = jnp.sum(qc * state, 0, keepdims=True)
        o_ref[:, h * RET_HEAD_DIM:(h + 1) * RET_HEAD_DIM] = inner * v + gam * o_x
        sn_ref[h] = gam * state + kc * v


def _ret_sample(q, k, v, state, gam_tile):
    b = q.shape[0]
    col = lambda a: a.reshape(b, RET_HEADS, RET_HEAD_DIM, 1)
    col_spec = pl.BlockSpec((None, RET_HEADS, RET_HEAD_DIM, 1), lambda i: (i, 0, 0, 0))
    row_spec = pl.BlockSpec((None, 1, RET_WIDTH), lambda i: (i, 0, 0))
    st_spec = pl.BlockSpec((None, RET_HEADS, RET_HEAD_DIM, RET_HEAD_DIM), lambda i: (i, 0, 0, 0))
    o, sn = pl.pallas_call(
        _ret_sample_kernel,
        grid=(b,),
        in_specs=[col_spec, col_spec, row_spec, st_spec,
                  pl.BlockSpec((RET_HEADS, 8, RET_HEAD_DIM), lambda i: (0, 0, 0))],
        out_specs=[row_spec, st_spec],
        out_shape=[jax.ShapeDtypeStruct((b, 1, RET_WIDTH), F32), jax.ShapeDtypeStruct(state.shape, F32)],
        compiler_params=_params("parallel"),
    )(col(q), col(k), v.reshape(b, 1, RET_WIDTH), state, gam_tile)
    return o.reshape(b, RET_WIDTH), sn


def _mixer_out_kernel(attn_ref, ret_ref, gate_ref, x_ref, gn_ref, wout_ref, n2_ref, wpq_ref, keys_ref,
                      h_ref, hn_ref, st_ref):
    gate = gate_ref[...]
    swish = gate * (1.0 / (1.0 + jnp.exp(-gate)))
    normed = []
    for hd in range(RET_HEADS):
        cols = slice(hd * RET_HEAD_DIM, (hd + 1) * RET_HEAD_DIM)
        r = ret_ref[:, cols]
        mu = jnp.mean(r, -1, keepdims=True)
        var = jnp.mean(jnp.square(r - mu), -1, keepdims=True)
        normed.append((r - mu) * lax.rsqrt(var + GN_EPS) * gn_ref[:, cols])
    ret_y = swish * jnp.concatenate(normed, -1)
    h = (x_ref[...]
         + jnp.dot(attn_ref[...].astype(MXU_DTYPE), wout_ref[0:ATTN_WIDTH, :], preferred_element_type=F32)
         + jnp.dot(ret_y.astype(MXU_DTYPE), wout_ref[ATTN_WIDTH:, :], preferred_element_type=F32))
    h_ref[...] = h
    hn = _rms(h, n2_ref[...])
    hn_ref[...] = hn
    hb = hn.astype(MXU_DTYPE)
    nt = (((1,), (1,)), ((), ()))
    for j in range(2 * PEER_HEADS):
        qj = jnp.dot(hb, wpq_ref[:, j * PEER_HALF:(j + 1) * PEER_HALF], preferred_element_type=F32)
        st_ref[j] = lax.dot_general(keys_ref[j], qj.astype(MXU_DTYPE), nt, preferred_element_type=F32)


def _mixer_out(attn, ret_o, gate, x, gn_g, w_out, n2_g, w_pq, keys):
    n = x.shape[0]
    tm = min(256, n)
    half = pl.BlockSpec((tm, ATTN_WIDTH), lambda i: (i, 0))
    full = pl.BlockSpec((tm, D_MODEL), lambda i: (i, 0))
    const = lambda shape: pl.BlockSpec(shape, lambda i: (0,) * len(shape))
    return pl.pallas_call(
        _mixer_out_kernel,
        grid=(n // tm,),
        in_specs=[half] * 3 + [full, const((1, RET_WIDTH)), const((D_MODEL, D_MODEL)), const((1, D_MODEL)),
                               const((D_MODEL, 2 * PEER_HEADS * PEER_HALF)),
                               const((2 * PEER_HEADS, PEER_N_KEYS, PEER_HALF))],
        out_specs=[full, full, pl.BlockSpec((2 * PEER_HEADS, PEER_N_KEYS, tm), lambda i: (0, 0, i))],
        out_shape=[jax.ShapeDtypeStruct((n, D_MODEL), F32), jax.ShapeDtypeStruct((n, D_MODEL), F32),
                   jax.ShapeDtypeStruct((2 * PEER_HEADS, PEER_N_KEYS, n), F32)],
        compiler_params=_params("parallel"),
    )(attn, ret_o, gate, x, gn_g.reshape(1, RET_WIDTH), w_out, n2_g.reshape(1, D_MODEL), w_pq, keys)


def _top_rows(s, k, payload=None):
    nrow = s.shape[0]
    ridx = lax.broadcasted_iota(jnp.int32, s.shape, 0)
    vals, picks = [], []
    for _ in range(k):
        m = jnp.max(s, 0, keepdims=True)
        am = jnp.min(jnp.where(s == m, ridx, nrow), 0, keepdims=True)
        hit = ridx == am
        vals.append(m)
        picks.append(am if payload is None else jnp.sum(jnp.where(hit, payload, 0), 0, keepdims=True))
        s = jnp.where(hit, -jnp.inf, s)
    return jnp.concatenate(vals, 0), jnp.concatenate(picks, 0)


def _product_grid(a0, a1, op):
    pieces = [op(a0[0:1], a1)]
    pieces += [op(a0[a:a + 1], a1[0:8]) for a in range(1, 8)]
    pieces.append(op(a0[8:16], a1[0:1]))
    return jnp.concatenate(pieces, 0)


def _topk_kernel(st_ref, eid_ref, g_ref):
    def head(hd, carry):
        sv0, si0 = _top_rows(st_ref[2 * hd], PEER_TOPK)
        sv1, si1 = _top_rows(st_ref[2 * hd + 1], PEER_TOPK)
        cand = _product_grid(sv0, sv1, jnp.add)
        cidx = _product_grid(si0, si1, lambda i, j: i * PEER_N_KEYS + j)
        best, eid = _top_rows(cand, PEER_TOPK, payload=cidx)
        e = jnp.exp(best - jnp.max(best, 0, keepdims=True))
        rows = pl.ds(pl.multiple_of(hd * PEER_TOPK, PEER_TOPK), PEER_TOPK)
        eid_ref[rows, :] = eid
        g_ref[rows, :] = e / jnp.sum(e, 0, keepdims=True)
        return carry

    lax.fori_loop(0, PEER_HEADS, head, 0)


def _topk(st):
    n = st.shape[-1]
    tn = LANES
    out_spec = pl.BlockSpec((PEER_SLOTS, tn), lambda i: (0, i))
    return pl.pallas_call(
        _topk_kernel,
        grid=(n // tn,),
        in_specs=[pl.BlockSpec((2 * PEER_HEADS, PEER_N_KEYS, tn), lambda i: (0, 0, i))],
        out_specs=[out_spec, out_spec],
        out_shape=[jax.ShapeDtypeStruct((PEER_SLOTS, n), jnp.int32), jax.ShapeDtypeStruct((PEER_SLOTS, n), F32)],
        compiler_params=_params("parallel"),
    )(st)


def _peer_kernel(eid_hbm, gt_ref, hn_ref, h_ref, nf_ref, uv_hbm, y_ref, eid_smem, buf, sems):
    step = pl.program_id(0)
    nchunk = PEER_TOKENS_PER_STEP // PEER_CHUNK
    nexp = PEER_CHUNK * PEER_SLOTS
    ids = pltpu.make_async_copy(eid_hbm.at[step], eid_smem, sems.at[2])
    ids.start()
    ids.wait()

    def gather(c, slot):
        def group(j, carry):
            for k in range(PEER_ISSUE_UNROLL):
                e = eid_smem[c * nexp + j * PEER_ISSUE_UNROLL + k]
                src = uv_hbm.at[pl.ds(pl.multiple_of(e * PEER_SLAB, PEER_SLAB), PEER_SLAB), :]
                dst = buf.at[slot, pl.ds((j * PEER_ISSUE_UNROLL + k) * PEER_PITCH, PEER_SLAB), :]
                pltpu.make_async_copy(src, dst, sems.at[slot]).start(priority=k % 2)
            return carry
        lax.fori_loop(0, nexp // PEER_ISSUE_UNROLL, group, 0)

    def wait(slot):
        span = pl.ds(0, nexp * PEER_SLAB)
        pltpu.make_async_copy(uv_hbm.at[span, :], buf.at[slot, span, :], sems.at[slot]).wait()

    def rows_of(slot, t, first_row):
        base = t * PEER_SLOTS * PEER_PITCH + first_row
        return jnp.concatenate(
            [jnp.concatenate([buf[slot, pl.ds(base + kg * 8 * PEER_PITCH + s, 8, stride=PEER_PITCH), :]
                              for s in range(D_MODEL // LANES)], axis=1)
             for kg in range(PEER_SLOTS // 8)], axis=0)

    tok_lane = lax.broadcasted_iota(jnp.int32, (PEER_SLOTS, PEER_TOKENS_PER_STEP), 1)
    gather(0, 0)

    def chunk(c, carry):
        slot = c % 2

        @pl.when(c + 1 < nchunk)
        def _():
            gather(c + 1, 1 - slot)

        wait(slot)
        for t in range(PEER_CHUNK):
            tok = c * PEER_CHUNK + t
            x = hn_ref[pl.ds(tok, 1), :]
            a = jax.nn.gelu(jnp.sum(rows_of(slot, t, 0) * x, -1, keepdims=True))
            gate = jnp.sum(jnp.where(tok_lane == tok, gt_ref[...], 0.0), -1, keepdims=True)
            out = jnp.sum((gate * a) * rows_of(slot, t, PEER_SLAB // 2), 0, keepdims=True)
            y_ref[pl.ds(tok, 1), :] = _rms(h_ref[pl.ds(tok, 1), :] + out, nf_ref[...])
        return carry

    lax.fori_loop(0, nchunk, chunk, 0)


def _expert_slabs(peer_u, peer_v):
    e = peer_u.shape[0]
    tiles = lambda a: a.reshape(e, D_MODEL // LANES, LANES)
    return jnp.concatenate([tiles(peer_u), tiles(peer_v)], axis=1).reshape(e * PEER_SLAB, LANES)


def _peer(eid_t, g_t, hn, h, nf_g, uv):
    n = hn.shape[0]
    tp = PEER_TOKENS_PER_STEP
    eid_tok = eid_t.T.reshape(n // tp, tp * PEER_SLOTS)
    rows = PEER_CHUNK * PEER_SLOTS * PEER_PITCH
    tok = pl.BlockSpec((tp, D_MODEL), lambda i: (i, 0))
    return pl.pallas_call(
        _peer_kernel,
        grid=(n // tp,),
        in_specs=[pl.BlockSpec(memory_space=pl.ANY),
                  pl.BlockSpec((PEER_SLOTS, tp), lambda i: (0, i)),
                  tok, tok,
                  pl.BlockSpec((1, D_MODEL), lambda i: (0, 0)),
                  pl.BlockSpec(memory_space=pl.ANY)],
        out_specs=tok,
        out_shape=jax.ShapeDtypeStruct((n, D_MODEL), F32),
        scratch_shapes=[pltpu.SMEM((tp * PEER_SLOTS,), jnp.int32),
                        pltpu.VMEM((2, rows, LANES), F32),
                        pltpu.SemaphoreType.DMA((3,))],
        compiler_params=_params("arbitrary"),
    )(eid_tok, g_t, hn, h, nf_g.reshape(1, D_MODEL), uv)


def _alibi_slopes():
    return (2.0 ** (-8.0 * (np.arange(ATTN_HEADS) + 1) / ATTN_HEADS)).astype(np.float32)


def _ret_consts():
    log_gamma = np.log(1.0 - 2.0 ** (-5.0 - np.arange(RET_HEADS))).astype(np.float32)
    pos = np.arange(RET_CHUNK, dtype=np.float32)
    rel = pos[:, None] - pos[None, :]
    dmat = np.where(rel[None] >= 0, np.exp(np.maximum(rel, 0.0)[None] * log_gamma[:, None, None]), 0.0)
    lanes = lambda col: np.broadcast_to(col[:, :, None], (RET_HEADS, col.shape[1], RET_HEAD_DIM))
    qdec = lanes(np.exp((pos[None, :] + 1.0) * log_gamma[:, None]))
    kdec = lanes(np.exp((RET_CHUNK - 1.0 - pos)[None, :] * log_gamma[:, None]))
    cdec = lanes(np.broadcast_to(np.exp(RET_CHUNK * log_gamma)[:, None], (RET_HEADS, 8)))
    gam = lanes(np.broadcast_to(np.exp(log_gamma)[:, None], (RET_HEADS, 8)))
    as_f32 = lambda a: jnp.asarray(np.ascontiguousarray(a, dtype=np.float32))
    return tuple(as_f32(a) for a in (dmat, qdec, kdec, cdec)), as_f32(gam)


def _channel_and_norm(attn, ret_o, gate, x, gn_g, w_out, n2_g, w_pq, keys, uv, nf_g):
    h, hn, st = _mixer_out(attn, ret_o, gate, x, gn_g, w_out, n2_g, w_pq, keys)
    eid_t, g_t = _topk(st)
    return _peer(eid_t, g_t, hn, h, nf_g, uv)


def kernel(x_prompt, x_sample, cache_k_win, cache_v_win, state_ret, norm1_g, w_in, ret_gn_g, w_out, norm2_g, w_pq, peer_sub_keys, peer_u, peer_v, norm_f_g):
    depth = w_in.shape[0]
    assert depth == 1, "single-layer stack"
    b, s, _ = x_prompt.shape
    bs = x_sample.shape[0]
    assert x_sample.shape[1] == 1 and s % (ATTN_BLOCK * max(DILATIONS)) == 0
    assert cache_k_win.shape[2] % (N_STEPS * max(DILATIONS)) == 0

    slopes = _alibi_slopes()
    slopes_tile = jnp.asarray(np.broadcast_to(slopes.reshape(ATTN_HEADS // 2, 2, 1, 1), (ATTN_HEADS // 2, 2, 8, LANES)).copy())
    ret_consts, gam_tile = _ret_consts()

    w_in_b = w_in[0].astype(MXU_DTYPE)
    w_out_b = w_out[0].astype(MXU_DTYPE)
    w_pq_b = w_pq[0].astype(MXU_DTYPE)
    keys_b = peer_sub_keys[0].reshape(2 * PEER_HEADS, PEER_N_KEYS, PEER_HALF).astype(MXU_DTYPE)
    uv = _expert_slabs(peer_u[0], peer_v[0])
    tail = (ret_gn_g[0], w_out_b, norm2_g[0], w_pq_b, keys_b, uv, norm_f_g)

    xp = x_prompt.reshape(b * s, D_MODEL)
    aq, ak, av, rq, rk, rv, rg, kt, vt = _inproj(xp, norm1_g[0], w_in_b, seq_len=s)
    seq = lambda a: a.reshape(b, s, ATTN_WIDTH)
    attn = _prompt_attention(seq(aq), seq(ak), seq(av), slopes_tile)
    ret_o, s_fin = _ret_prompt(seq(rq), seq(rk), seq(rv), ret_consts)
    y_prompt = _channel_and_norm(attn.reshape(b * s, ATTN_WIDTH), ret_o.reshape(b * s, RET_WIDTH), rg, xp, *tail)
    keep = min(N_STEPS * max(DILATIONS), s)
    window = lambda a: a.reshape(b, ATTN_HEADS, ATTN_HEAD_DIM, s).transpose(0, 3, 1, 2)[:, s - keep:]
    k_win, v_win = window(kt), window(vt)

    xs = x_sample.reshape(bs, D_MODEL)
    aq, ak, av, rq, rk, rv, rg = _inproj(xs, norm1_g[0], w_in_b)
    position_minor = lambda c: c.transpose(0, 2, 3, 1)
    attn = _sample_attention(aq, ak, av, position_minor(cache_k_win[0]), position_minor(cache_v_win[0]))
    ret_o, s_new = _ret_sample(rq, rk, rv, state_ret[0], gam_tile)
    y_sample = _channel_and_norm(attn, ret_o, rg, xs, *tail)

    return (y_prompt.reshape(b, s, D_MODEL), y_sample.reshape(bs, 1, D_MODEL),
            k_win[None], v_win[None], s_fin[None],
            ak.reshape(1, bs, 1, ATTN_HEADS, ATTN_HEAD_DIM), av.reshape(1, bs, 1, ATTN_HEADS, ATTN_HEAD_DIM),
            s_new[None])
```

```python
import functools

import numpy as np
import jax
import jax.numpy as jnp
from jax import lax
from jax.experimental import pallas as pl
from jax.experimental.pallas import tpu as pltpu
from jax.experimental.pallas import tpu_sc as plsc

F32 = jnp.float32
MXU_DTYPE = jnp.bfloat16

D_MODEL = 1024
ATTN_WIDTH = 512
RET_WIDTH = 512
ATTN_HEAD_DIM = 64
ATTN_HEADS = 8
RET_HEADS = 4
RET_HEAD_DIM = 128
DILATIONS = (1, 4, 16)
N_STEPS = 128
ATTN_BLOCK = 128
ATTN_PAD = N_STEPS * max(DILATIONS)
RET_CHUNK = 128
PEER_HEADS = 8
PEER_N_KEYS = 128
PEER_TOPK = 16
PEER_HALF = 128
PEER_SLOTS = PEER_HEADS * PEER_TOPK
NORM_EPS = 1e-6
GN_EPS = 1e-5
IN_SPLITS = 7
RET_K_SCALE = RET_HEAD_DIM ** -0.5
ATTN_SCALE = ATTN_HEAD_DIM ** -0.5
LANES = 128
VMEM_LIMIT = 48 * 1024 * 1024

PEER_TOKENS_PER_STEP = 128
PEER_CHUNK = 8
PEER_SLAB = 2 * D_MODEL // LANES
PEER_PITCH = PEER_SLAB + 1
PEER_ISSUE_UNROLL = 8
SC_TOKENS = 8192
SC_ROWS = 32


def _params(*sem):
    return pltpu.CompilerParams(dimension_semantics=sem, vmem_limit_bytes=VMEM_LIMIT)


def _rms(x, g):
    r = lax.rsqrt(jnp.mean(x * x, axis=-1, keepdims=True) + NORM_EPS)
    return x * r * g


def _inproj_kernel(x_ref, g_ref, w_ref, *refs):
    out_refs = refs[-IN_SPLITS:] if len(refs) == IN_SPLITS else refs[1:1 + IN_SPLITS]
    xn = _rms(x_ref[...], g_ref[...]).astype(MXU_DTYPE)
    for i, o_ref in enumerate(out_refs):
        y = jnp.dot(xn, w_ref[:, i * ATTN_WIDTH:(i + 1) * ATTN_WIDTH], preferred_element_type=F32)
        if i == 4:
            y = y * RET_K_SCALE
        o_ref[...] = y
    if len(refs) > IN_SPLITS:
        wkv_t_ref, kt_ref, vt_ref = refs[0], refs[-2], refs[-1]
        nt = (((1,), (1,)), ((), ()))
        kv_t = lax.dot_general(wkv_t_ref[...], xn, nt, preferred_element_type=F32)
        kt_ref[...] = kv_t[:ATTN_WIDTH]
        vt_ref[...] = kv_t[ATTN_WIDTH:]


def _inproj(x, g, w, seq_len=None):
    n = x.shape[0]
    tm = min(256, n)
    out = jax.ShapeDtypeStruct((n, ATTN_WIDTH), F32)
    in_specs = [pl.BlockSpec((tm, D_MODEL), lambda i: (i, 0)),
                pl.BlockSpec((1, D_MODEL), lambda i: (0, 0)),
                pl.BlockSpec((D_MODEL, IN_SPLITS * ATTN_WIDTH), lambda i: (0, 0))]
    out_specs = [pl.BlockSpec((tm, ATTN_WIDTH), lambda i: (i, 0))] * IN_SPLITS
    out_shape = [out] * IN_SPLITS
    args = [x, g.reshape(1, D_MODEL), w]
    if seq_len is not None:
        per_row = seq_len // tm
        in_specs.append(pl.BlockSpec((2 * ATTN_WIDTH, D_MODEL), lambda i: (0, 0)))
        args.append(w[:, ATTN_WIDTH:3 * ATTN_WIDTH].T)
        out_specs += [pl.BlockSpec((None, ATTN_WIDTH, tm), lambda i: (i // per_row, 0, i % per_row))] * 2
        out_shape += [jax.ShapeDtypeStruct((n // seq_len, ATTN_WIDTH, seq_len), F32)] * 2
    return pl.pallas_call(
        _inproj_kernel,
        grid=(n // tm,),
        in_specs=in_specs,
        out_specs=out_specs,
        out_shape=out_shape,
        compiler_params=_params("parallel"),
    )(*args)


def _merge_groups(outs, lses):
    m = functools.reduce(jnp.maximum, lses)
    es = [jnp.exp(l - m) for l in lses]
    den = functools.reduce(jnp.add, es)
    return functools.reduce(jnp.add, [(e / den) * o for e, o in zip(es, outs)])


def _prompt_attn_kernel(s_len, q_ref, k_ref, v_ref, slope_ref, o_ref, og_ref, lg_ref):
    lane = lax.broadcasted_iota(jnp.int32, (ATTN_BLOCK, LANES), 1)
    first = lane < ATTN_HEAD_DIM
    qi = lax.broadcasted_iota(jnp.int32, (ATTN_BLOCK, 2 * ATTN_BLOCK), 0)
    kj = lax.broadcasted_iota(jnp.int32, (ATTN_BLOCK, 2 * ATTN_BLOCK), 1)
    back = qi + ATTN_BLOCK - kj
    in_band = (back >= 0) & (back <= N_STEPS)
    nt = (((1,), (1,)), ((), ()))
    ntile = s_len // ATTN_BLOCK
    for g, dil in enumerate(DILATIONS):
        dist = (back * dil).astype(F32)
        bias = [jnp.where(in_band, -slope_ref[hh, 0:1, 0:1] * dist, -jnp.inf) for hh in range(2)]

        def tile(t, carry, g=g, dil=dil, bias=bias):
            r, n = t % dil, t // dil
            qrow = r + n * (dil * ATTN_BLOCK)
            krow = ATTN_PAD + qrow - dil * ATTN_BLOCK
            q = q_ref[pl.ds(qrow, ATTN_BLOCK, stride=dil), :]
            kb = k_ref[pl.ds(krow, 2 * ATTN_BLOCK, stride=dil), :].astype(MXU_DTYPE)
            vb = v_ref[pl.ds(krow, 2 * ATTN_BLOCK, stride=dil), :].astype(MXU_DTYPE)
            exists = kj + n * ATTN_BLOCK >= ATTN_BLOCK
            outs, lses = [], []
            for hh in range(2):
                qm = jnp.where(first if hh == 0 else ~first, q, 0.0).astype(MXU_DTYPE)
                sc = lax.dot_general(qm, kb, nt, preferred_element_type=F32) * ATTN_SCALE + bias[hh]
                sc = jnp.where(exists, sc, -jnp.inf)
                m = jnp.max(sc, -1, keepdims=True)
                p = jnp.exp(sc - m)
                den = jnp.sum(p, -1, keepdims=True)
                outs.append(jnp.dot((p / den).astype(MXU_DTYPE), vb, preferred_element_type=F32))
                lses.append(jnp.broadcast_to(m + jnp.log(den), (ATTN_BLOCK, LANES)))
            og_ref[g, pl.ds(qrow, ATTN_BLOCK, stride=dil), :] = jnp.where(first, outs[0], outs[1])
            lg_ref[g, pl.ds(qrow, ATTN_BLOCK, stride=dil), :] = jnp.where(first, lses[0], lses[1])
            return carry

        lax.fori_loop(0, ntile, tile, 0, unroll=2)

    def merge(i, carry):
        rows = pl.ds(pl.multiple_of(i * ATTN_BLOCK, ATTN_BLOCK), ATTN_BLOCK)
        groups = range(len(DILATIONS))
        o_ref[rows, :] = _merge_groups([og_ref[g, rows, :] for g in groups], [lg_ref[g, rows, :] for g in groups])
        return carry

    lax.fori_loop(0, ntile, merge, 0)


def _prompt_attention(q, k, v, slopes_tile):
    b, s, _ = q.shape
    pad = lambda a: jnp.pad(a, ((0, 0), (ATTN_PAD, 0), (0, 0)))
    q_spec = pl.BlockSpec((None, s, LANES), lambda i, h: (i, 0, h))
    kv_spec = pl.BlockSpec((None, ATTN_PAD + s, LANES), lambda i, h: (i, 0, h))
    return pl.pallas_call(
        functools.partial(_prompt_attn_kernel, s),
        grid=(b, ATTN_WIDTH // LANES),
        in_specs=[q_spec, kv_spec, kv_spec, pl.BlockSpec((None, 2, 8, LANES), lambda i, h: (h, 0, 0, 0))],
        out_specs=q_spec,
        out_shape=jax.ShapeDtypeStruct((b, s, ATTN_WIDTH), F32),
        scratch_shapes=[pltpu.VMEM((len(DILATIONS), s, LANES), F32), pltpu.VMEM((len(DILATIONS), s, LANES), F32)],
        compiler_params=_params("parallel", "parallel"),
    )(q, pad(k), pad(v), slopes_tile)


def _sample_attn_kernel(slopes, q_ref, kn_ref, vn_ref, kt_ref, vt_ref, o_ref):
    w = kt_ref.shape[-1]
    back_i = w - lax.broadcasted_iota(jnp.int32, (1, w), 1)
    back = back_i.astype(F32)
    valid = [(back_i % dil == 0) & (back_i <= N_STEPS * dil) for dil in DILATIONS]
    for hd in range(ATTN_HEADS):
        q = q_ref[hd]
        raw = jnp.sum(kt_ref[hd] * q, 0, keepdims=True) * ATTN_SCALE - slopes[hd] * back
        l0 = jnp.sum(q * kn_ref[hd], 0, keepdims=True) * ATTN_SCALE
        vt = vt_ref[hd]
        outs, lses = [], []
        for ok in valid:
            m = jnp.maximum(jnp.max(jnp.where(ok, raw, -jnp.inf), -1, keepdims=True), l0)
            p = jnp.where(ok, jnp.exp(raw - m), 0.0)
            p0 = jnp.exp(l0 - m)
            s = jnp.sum(p, -1, keepdims=True) + p0
            outs.append((jnp.sum(vt * p, -1, keepdims=True) + p0 * vn_ref[hd]) / s)
            lses.append(m + jnp.log(s))
        o_ref[hd] = _merge_groups(outs, lses)


def _sample_attention(q, kn, vn, cache_kt, cache_vt):
    b = q.shape[0]
    assert cache_kt.shape[-1] >= N_STEPS * max(DILATIONS)
    cols = lambda a: a.reshape(b, ATTN_HEADS, ATTN_HEAD_DIM, 1)
    col_spec = pl.BlockSpec((None, ATTN_HEADS, ATTN_HEAD_DIM, 1), lambda i: (i, 0, 0, 0))
    win_spec = pl.BlockSpec((None,) + cache_kt.shape[1:], lambda i: (i, 0, 0, 0))
    out = pl.pallas_call(
        functools.partial(_sample_attn_kernel, [float(v) for v in _alibi_slopes()]),
        grid=(b,),
        in_specs=[col_spec, col_spec, col_spec, win_spec, win_spec],
        out_specs=col_spec,
        out_shape=jax.ShapeDtypeStruct((b, ATTN_HEADS, ATTN_HEAD_DIM, 1), F32),
        compiler_params=_params("parallel"),
    )(cols(q), cols(kn), cols(vn), cache_kt, cache_vt)
    return out.reshape(b, ATTN_WIDTH)


def _ret_prompt_kernel(nchunk, q_ref, k_ref, v_ref, dmat_ref, qdec_ref, kdec_ref, cdec_ref, o_ref, s_ref):
    s_ref[...] = jnp.zeros_like(s_ref)
    nt = (((1,), (1,)), ((), ()))
    tn = (((0,), (0,)), ((), ()))

    def chunk(c, carry):
        row = pl.multiple_of(c * RET_CHUNK, RET_CHUNK)
        q = q_ref[pl.ds(row, RET_CHUNK), :]
        k = k_ref[pl.ds(row, RET_CHUNK), :]
        v = v_ref[pl.ds(row, RET_CHUNK), :].astype(MXU_DTYPE)
        qb = q.astype(MXU_DTYPE)
        state = s_ref[...]
        inner = lax.dot_general(qb, k.astype(MXU_DTYPE), nt, preferred_element_type=F32) * dmat_ref[...]
        o_in = jnp.dot(inner.astype(MXU_DTYPE), v, preferred_element_type=F32)
        o_x = jnp.dot(qb, state.astype(MXU_DTYPE), preferred_element_type=F32) * qdec_ref[...]
        o_ref[pl.ds(row, RET_CHUNK), :] = o_in + o_x
        kd = (k * kdec_ref[...]).astype(MXU_DTYPE)
        s_ref[...] = cdec_ref[0:1, :] * state + lax.dot_general(kd, v, tn, preferred_element_type=F32)
        return carry

    lax.fori_loop(0, nchunk, chunk, 0)


def _ret_prompt(q, k, v, consts):
    b, s, _ = q.shape
    seq = pl.BlockSpec((None, s, RET_HEAD_DIM), lambda i, h: (i, 0, h))
    per_head = lambda rows: pl.BlockSpec((None, rows, RET_HEAD_DIM), lambda i, h: (h, 0, 0))
    return pl.pallas_call(
        functools.partial(_ret_prompt_kernel, s // RET_CHUNK),
        grid=(b, RET_HEADS),
        in_specs=[seq, seq, seq, per_head(RET_CHUNK), per_head(RET_CHUNK), per_head(RET_CHUNK), per_head(8)],
        out_specs=[seq, pl.BlockSpec((None, None, RET_HEAD_DIM, RET_HEAD_DIM), lambda i, h: (i, h, 0, 0))],
        out_shape=[jax.ShapeDtypeStruct((b, s, RET_WIDTH), F32),
                   jax.ShapeDtypeStruct((b, RET_HEADS, RET_HEAD_DIM, RET_HEAD_DIM), F32)],
        compiler_params=_params("parallel", "parallel"),
    )(q, k, v, *consts)


def _ret_sample_kernel(qc_ref, kc_ref, v_ref, s_ref, gam_ref, o_ref, sn_ref):
    for h in range(RET_HEADS):
        qc = qc_ref[h]
        kc = kc_ref[h]
        v = v_ref[:, h * RET_HEAD_DIM:(h + 1) * RET_HEAD_DIM]
        gam = gam_ref[h, 0:1, :]
        state = s_ref[h]
        inner = jnp.sum(qc * kc, 0, keepdims=True)
        o_x = jnp.sum(qc * state, 0, keepdims=True)
        o_ref[:, h * RET_HEAD_DIM:(h + 1) * RET_HEAD_DIM] = inner * v + gam * o_x
        sn_ref[h] = gam * state + kc * v


def _ret_sample(q, k, v, state, gam_tile):
    b = q.shape[0]
    col = lambda a: a.reshape(b, RET_HEADS, RET_HEAD_DIM, 1)
    col_spec = pl.BlockSpec((None, RET_HEADS, RET_HEAD_DIM, 1), lambda i: (i, 0, 0, 0))
    row_spec = pl.BlockSpec((None, 1, RET_WIDTH), lambda i: (i, 0, 0))
    st_spec = pl.BlockSpec((None, RET_HEADS, RET_HEAD_DIM, RET_HEAD_DIM), lambda i: (i, 0, 0, 0))
    o, sn = pl.pallas_call(
        _ret_sample_kernel,
        grid=(b,),
        in_specs=[col_spec, col_spec, row_spec, st_spec,
                  pl.BlockSpec((RET_HEADS, 8, RET_HEAD_DIM), lambda i: (0, 0, 0))],
        out_specs=[row_spec, st_spec],
        out_shape=[jax.ShapeDtypeStruct((b, 1, RET_WIDTH), F32), jax.ShapeDtypeStruct(state.shape, F32)],
        compiler_params=_params("parallel"),
    )(col(q), col(k), v.reshape(b, 1, RET_WIDTH), state, gam_tile)
    return o.reshape(b, RET_WIDTH), sn


def _mixer_out_kernel(attn_ref, ret_ref, gate_ref, x_ref, gn_ref, wout_ref, n2_ref, wpq_ref, keys_ref,
                      h_ref, hn_ref, st_ref):
    gate = gate_ref[...]
    swish = gate * (1.0 / (1.0 + jnp.exp(-gate)))
    normed = []
    for hd in range(RET_HEADS):
        cols = slice(hd * RET_HEAD_DIM, (hd + 1) * RET_HEAD_DIM)
        r = ret_ref[:, cols]
        mu = jnp.mean(r, -1, keepdims=True)
        var = jnp.mean(jnp.square(r - mu), -1, keepdims=True)
        normed.append((r - mu) * lax.rsqrt(var + GN_EPS) * gn_ref[:, cols])
    ret_y = swish * jnp.concatenate(normed, -1)
    h = (x_ref[...]
         + jnp.dot(attn_ref[...].astype(MXU_DTYPE), wout_ref[0:ATTN_WIDTH, :], preferred_element_type=F32)
         + jnp.dot(ret_y.astype(MXU_DTYPE), wout_ref[ATTN_WIDTH:, :], preferred_element_type=F32))
    h_ref[...] = h
    hn = _rms(h, n2_ref[...])
    hn_ref[...] = hn
    hb = hn.astype(MXU_DTYPE)
    nt = (((1,), (1,)), ((), ()))
    for j in range(2 * PEER_HEADS):
        qj = jnp.dot(hb, wpq_ref[:, j * PEER_HALF:(j + 1) * PEER_HALF], preferred_element_type=F32)
        st_ref[j] = lax.dot_general(keys_ref[j], qj.astype(MXU_DTYPE), nt, preferred_element_type=F32)


def _mixer_out(attn, ret_o, gate, x, gn_g, w_out, n2_g, w_pq, keys):
    n = x.shape[0]
    tm = min(256, n)
    half = pl.BlockSpec((tm, ATTN_WIDTH), lambda i: (i, 0))
    full = pl.BlockSpec((tm, D_MODEL), lambda i: (i, 0))
    const = lambda shape: pl.BlockSpec(shape, lambda i: (0,) * len(shape))
    return pl.pallas_call(
        _mixer_out_kernel,
        grid=(n // tm,),
        in_specs=[half] * 3 + [full, const((1, RET_WIDTH)), const((D_MODEL, D_MODEL)), const((1, D_MODEL)),
                               const((D_MODEL, 2 * PEER_HEADS * PEER_HALF)),
                               const((2 * PEER_HEADS, PEER_N_KEYS, PEER_HALF))],
        out_specs=[full, full, pl.BlockSpec((2 * PEER_HEADS, PEER_N_KEYS, tm), lambda i: (0, 0, i))],
        out_shape=[jax.ShapeDtypeStruct((n, D_MODEL), F32), jax.ShapeDtypeStruct((n, D_MODEL), F32),
                   jax.ShapeDtypeStruct((2 * PEER_HEADS, PEER_N_KEYS, n), F32)],
        compiler_params=_params("parallel"),
    )(attn, ret_o, gate, x, gn_g.reshape(1, RET_WIDTH), w_out, n2_g.reshape(1, D_MODEL), w_pq, keys)


def _top_rows(s, k, payload=None):
    nrow = s.shape[0]
    ridx = lax.broadcasted_iota(jnp.int32, s.shape, 0)
    vals, picks = [], []
    for _ in range(k):
        m = jnp.max(s, 0, keepdims=True)
        am = jnp.min(jnp.where(s == m, ridx, nrow), 0, keepdims=True)
        hit = ridx == am
        vals.append(m)
        picks.append(am if payload is None else jnp.sum(jnp.where(hit, payload, 0), 0, keepdims=True))
        s = jnp.where(hit, -jnp.inf, s)
    return jnp.concatenate(vals, 0), jnp.concatenate(picks, 0)


def _product_grid(a0, a1, op):
    pieces = [op(a0[0:1], a1)]
    pieces += [op(a0[a:a + 1], a1[0:8]) for a in range(1, 8)]
    pieces.append(op(a0[8:16], a1[0:1]))
    return jnp.concatenate(pieces, 0)


def _topk_kernel(st_ref, eid_ref, g_ref):
    def head(hd, carry):
        sv0, si0 = _top_rows(st_ref[2 * hd], PEER_TOPK)
        sv1, si1 = _top_rows(st_ref[2 * hd + 1], PEER_TOPK)
        cand = _product_grid(sv0, sv1, jnp.add)
        cidx = _product_grid(si0, si1, lambda i, j: i * PEER_N_KEYS + j)
        best, eid = _top_rows(cand, PEER_TOPK, payload=cidx)
        e = jnp.exp(best - jnp.max(best, 0, keepdims=True))
        rows = pl.ds(pl.multiple_of(hd * PEER_TOPK, PEER_TOPK), PEER_TOPK)
        eid_ref[rows, :] = eid
        g_ref[rows, :] = e / jnp.sum(e, 0, keepdims=True)
        return carry

    lax.fori_loop(0, PEER_HEADS, head, 0)


def _topk(st):
    n = st.shape[-1]
    tn = LANES
    out_spec = pl.BlockSpec((PEER_SLOTS, tn), lambda i: (0, i))
    return pl.pallas_call(
        _topk_kernel,
        grid=(n // tn,),
        in_specs=[pl.BlockSpec((2 * PEER_HEADS, PEER_N_KEYS, tn), lambda i: (0, 0, i))],
        out_specs=[out_spec, out_spec],
        out_shape=[jax.ShapeDtypeStruct((PEER_SLOTS, n), jnp.int32), jax.ShapeDtypeStruct((PEER_SLOTS, n), F32)],
        compiler_params=_params("parallel"),
    )(st)


def _peer_kernel(first_step, eid_hbm, gt_ref, hn_ref, h_ref, nf_ref, uv_hbm, y_ref, eid_smem, buf, sems):
    step = pl.program_id(0) + first_step
    nchunk = PEER_TOKENS_PER_STEP // PEER_CHUNK
    nexp = PEER_CHUNK * PEER_SLOTS
    ids = pltpu.make_async_copy(eid_hbm.at[step], eid_smem, sems.at[2])
    ids.start()
    ids.wait()

    def start_copy(c, i, slot, k):
        e = eid_smem[c * nexp + i]
        src = uv_hbm.at[pl.ds(pl.multiple_of(e * PEER_SLAB, PEER_SLAB), PEER_SLAB), :]
        dst = buf.at[slot, pl.ds(i * PEER_PITCH, PEER_SLAB), :]
        pltpu.make_async_copy(src, dst, sems.at[slot]).start(priority=k % 2)

    def gather(c, slot):
        def group(j, carry):
            for k in range(PEER_ISSUE_UNROLL):
                start_copy(c, j * PEER_ISSUE_UNROLL + k, slot, k)
            return carry
        lax.fori_loop(0, nexp // PEER_ISSUE_UNROLL, group, 0)

    def wait(slot):
        span = pl.ds(0, nexp * PEER_SLAB)
        pltpu.make_async_copy(uv_hbm.at[span, :], buf.at[slot, span, :], sems.at[slot]).wait()

    def rows_of(slot, t, first_row):
        base = t * PEER_SLOTS * PEER_PITCH + first_row
        return jnp.concatenate(
            [jnp.concatenate([buf[slot, pl.ds(base + kg * 8 * PEER_PITCH + s, 8, stride=PEER_PITCH), :]
                              for s in range(D_MODEL // LANES)], axis=1)
             for kg in range(PEER_SLOTS // 8)], axis=0)

    tok_lane = lax.broadcasted_iota(jnp.int32, (PEER_SLOTS, PEER_TOKENS_PER_STEP), 1)
    gather(0, 0)

    def chunk(c, carry):
        slot = c % 2

        @pl.when(c + 1 < nchunk)
        def _():
            gather(c + 1, 1 - slot)

        wait(slot)
        for t in range(PEER_CHUNK):
            tok = c * PEER_CHUNK + t
            x = hn_ref[pl.ds(tok, 1), :]
            a = jax.nn.gelu(jnp.sum(rows_of(slot, t, 0) * x, -1, keepdims=True))
            gate = jnp.sum(jnp.where(tok_lane == tok, gt_ref[...], 0.0), -1, keepdims=True)
            out = jnp.sum((gate * a) * rows_of(slot, t, PEER_SLAB // 2), 0, keepdims=True)
            y_ref[pl.ds(tok, 1), :] = _rms(h_ref[pl.ds(tok, 1), :] + out, nf_ref[...])
        return carry

    lax.fori_loop(0, nchunk, chunk, 0)


def _expert_slabs(peer_u, peer_v):
    e = peer_u.shape[0]
    tiles = lambda a: a.reshape(e, D_MODEL // LANES, LANES)
    return jnp.concatenate([tiles(peer_u), tiles(peer_v)], axis=1).reshape(e * PEER_SLAB, LANES)


def _peer_sc_experts(eid_flat, g_flat, hn, uv_rows, n_sc):
    info = plsc.get_sparse_core_info()
    nw = info.num_cores * info.num_subcores
    nl = info.num_lanes
    per_w = n_sc // nw
    quarter = SC_ROWS
    mesh = plsc.VectorSubcoreMesh(core_axis_name="c", subcore_axis_name="s")

    @functools.partial(
        pl.kernel, mesh=mesh,
        out_type=jax.ShapeDtypeStruct((n_sc, D_MODEL), F32),
        scratch_types=[pltpu.VMEM((PEER_SLOTS,), jnp.int32), pltpu.VMEM((PEER_SLOTS,), F32),
                       pltpu.VMEM((D_MODEL,), F32), pltpu.VMEM((quarter, 2 * D_MODEL), F32),
                       pltpu.VMEM((D_MODEL,), F32), pltpu.VMEM((quarter,), F32),
                       pltpu.SemaphoreType.DMA],
        compiler_params=pltpu.CompilerParams(needs_layout_passes=False),
    )
    def k(eid_hbm, g_hbm, hn_hbm, uv_hbm, out_hbm, idx_v, g_v, x_v, rows_v, o_v, w_v, sem):
        wid = lax.axis_index("s") * info.num_cores + lax.axis_index("c")
        lane = lax.iota(jnp.int32, nl)
        nvec = D_MODEL // nl

        @pl.loop(0, per_w)
        def _(ti):
            tok = wid * per_w + ti
            base = pl.multiple_of(tok * PEER_SLOTS, PEER_SLOTS)
            pltpu.sync_copy(eid_hbm.at[pl.ds(base, PEER_SLOTS)], idx_v)
            pltpu.sync_copy(g_hbm.at[pl.ds(base, PEER_SLOTS)], g_v)
            pltpu.sync_copy(hn_hbm.at[tok], x_v)
            for c in range(nvec):
                o_v[pl.ds(c * nl, nl)] = jnp.zeros((nl,), F32)
            for q in range(PEER_SLOTS // quarter):
                pltpu.async_copy(uv_hbm.at[idx_v.at[pl.ds(q * quarter, quarter)]], rows_v, sem).wait()
                for half in range(quarter // nl):
                    def dot_row(r, a_vec, half=half):
                        row = half * nl + r
                        accs = [jnp.zeros((nl,), F32) for _ in range(4)]
                        for c in range(nvec):
                            accs[c % 4] = accs[c % 4] + rows_v[row, pl.ds(c * nl, nl)] * x_v[pl.ds(c * nl, nl)]
                        total = jnp.sum((accs[0] + accs[1]) + (accs[2] + accs[3]))
                        return jnp.where(lane == r, total, a_vec)
                    a = lax.fori_loop(0, nl, dot_row, jnp.zeros((nl,), F32))
                    z = 0.7978845608028654 * (a + 0.044715 * (a * a * a))
                    tanh_z = 1.0 - 2.0 / (1.0 + jnp.exp(2.0 * z))
                    gelu = 0.5 * a * (1.0 + tanh_z)
                    w_v[pl.ds(half * nl, nl)] = g_v[pl.ds(q * quarter + half * nl, nl)] * gelu

                @pl.loop(0, quarter)
                def _(r):
                    wv = plsc.load_gather(w_v, [jnp.full((nl,), r, jnp.int32)])
                    for c in range(nvec):
                        plsc.addupdate(o_v.at[pl.ds(c * nl, nl)], wv * rows_v[r, pl.ds(D_MODEL + c * nl, nl)])
            pltpu.sync_copy(o_v, out_hbm.at[tok])

    return k(eid_flat, g_flat, hn, uv_rows)


def _residual_norm_kernel(h_ref, p_ref, nf_ref, y_ref):
    y_ref[...] = _rms(h_ref[...] + p_ref[...], nf_ref[...])


def _peer(eid_t, g_t, hn, h, nf_g, uv):
    n = hn.shape[0]
    tp = PEER_TOKENS_PER_STEP
    n_sc = SC_TOKENS if n >= 4 * SC_TOKENS else 0
    eid_rows = eid_t.T
    eid_tok = eid_rows.reshape(n // tp, tp * PEER_SLOTS)
    rows = PEER_CHUNK * PEER_SLOTS * PEER_PITCH
    first = n_sc // tp
    tok = pl.BlockSpec((tp, D_MODEL), lambda i: (i + first, 0))
    y_tc = pl.pallas_call(
        functools.partial(_peer_kernel, first),
        grid=((n - n_sc) // tp,),
        in_specs=[pl.BlockSpec(memory_space=pl.ANY),
                  pl.BlockSpec((PEER_SLOTS, tp), lambda i: (0, i + first)),
                  tok, tok,
                  pl.BlockSpec((1, D_MODEL), lambda i: (0, 0)),
                  pl.BlockSpec(memory_space=pl.ANY)],
        out_specs=pl.BlockSpec((tp, D_MODEL), lambda i: (i, 0)),
        out_shape=jax.ShapeDtypeStruct((n - n_sc, D_MODEL), F32),
        scratch_shapes=[pltpu.SMEM((tp * PEER_SLOTS,), jnp.int32),
                        pltpu.VMEM((2, rows, LANES), F32),
                        pltpu.SemaphoreType.DMA((3,))],
        compiler_params=_params("arbitrary"),
    )(eid_tok, g_t, hn, h, nf_g.reshape(1, D_MODEL), uv)
    if n_sc == 0:
        return y_tc
    p_sc = _peer_sc_experts(eid_rows.reshape(-1), g_t.T.reshape(-1), hn,
                            uv.reshape(-1, 2 * D_MODEL), n_sc)
    tm = 256
    blk = pl.BlockSpec((tm, D_MODEL), lambda i: (i, 0))
    y_sc = pl.pallas_call(
        _residual_norm_kernel,
        grid=(n_sc // tm,),
        in_specs=[blk, blk, pl.BlockSpec((1, D_MODEL), lambda i: (0, 0))],
        out_specs=blk,
        out_shape=jax.ShapeDtypeStruct((n_sc, D_MODEL), F32),
        compiler_params=_params("parallel"),
    )(h, p_sc, nf_g.reshape(1, D_MODEL))
    return jnp.concatenate([y_sc, y_tc], axis=0)


def _alibi_slopes():
    return (2.0 ** (-8.0 * (np.arange(ATTN_HEADS) + 1) / ATTN_HEADS)).astype(np.float32)


def _ret_consts():
    log_gamma = np.log(1.0 - 2.0 ** (-5.0 - np.arange(RET_HEADS))).astype(np.float32)
    pos = np.arange(RET_CHUNK, dtype=np.float32)
    rel = pos[:, None] - pos[None, :]
    dmat = np.where(rel[None] >= 0, np.exp(np.maximum(rel, 0.0)[None] * log_gamma[:, None, None]), 0.0)
    lanes = lambda col: np.broadcast_to(col[:, :, None], (RET_HEADS, col.shape[1], RET_HEAD_DIM))
    qdec = lanes(np.exp((pos[None, :] + 1.0) * log_gamma[:, None]))
    kdec = lanes(np.exp((RET_CHUNK - 1.0 - pos)[None, :] * log_gamma[:, None]))
    cdec = lanes(np.broadcast_to(np.exp(RET_CHUNK * log_gamma)[:, None], (RET_HEADS, 8)))
    gam = lanes(np.broadcast_to(np.exp(log_gamma)[:, None], (RET_HEADS, 8)))
    as_f32 = lambda a: jnp.asarray(np.ascontiguousarray(a, dtype=np.float32))
    return tuple(as_f32(a) for a in (dmat, qdec, kdec, cdec)), as_f32(gam)


def _channel_and_norm(attn, ret_o, gate, x, gn_g, w_out, n2_g, w_pq, keys, uv, nf_g):
    h, hn, st = _mixer_out(attn, ret_o, gate, x, gn_g, w_out, n2_g, w_pq, keys)
    eid_t, g_t = _topk(st)
    return _peer(eid_t, g_t, hn, h, nf_g, uv)


def kernel(x_prompt, x_sample, cache_k_win, cache_v_win, state_ret, norm1_g, w_in, ret_gn_g, w_out, norm2_g, w_pq, peer_sub_keys, peer_u, peer_v, norm_f_g):
    depth = w_in.shape[0]
    assert depth == 1, "single-layer stack"
    b, s, _ = x_prompt.shape
    bs = x_sample.shape[0]
    assert x_sample.shape[1] == 1 and s % (ATTN_BLOCK * max(DILATIONS)) == 0
    assert cache_k_win.shape[2] % (N_STEPS * max(DILATIONS)) == 0

    slopes = _alibi_slopes()
    slopes_tile = jnp.asarray(np.broadcast_to(slopes.reshape(ATTN_HEADS // 2, 2, 1, 1), (ATTN_HEADS // 2, 2, 8, LANES)).copy())
    ret_consts, gam_tile = _ret_consts()

    w_in_b = w_in[0].astype(MXU_DTYPE)
    w_out_b = w_out[0].astype(MXU_DTYPE)
    w_pq_b = w_pq[0].astype(MXU_DTYPE)
    keys_b = peer_sub_keys[0].reshape(2 * PEER_HEADS, PEER_N_KEYS, PEER_HALF).astype(MXU_DTYPE)
    uv = _expert_slabs(peer_u[0], peer_v[0])
    tail = (ret_gn_g[0], w_out_b, norm2_g[0], w_pq_b, keys_b, uv, norm_f_g)

    xp = x_prompt.reshape(b * s, D_MODEL)
    aq, ak, av, rq, rk, rv, rg, kt, vt = _inproj(xp, norm1_g[0], w_in_b, seq_len=s)
    seq = lambda a: a.reshape(b, s, ATTN_WIDTH)
    attn = _prompt_attention(seq(aq), seq(ak), seq(av), slopes_tile)
    ret_o, s_fin = _ret_prompt(seq(rq), seq(rk), seq(rv), ret_consts)
    y_prompt = _channel_and_norm(attn.reshape(b * s, ATTN_WIDTH), ret_o.reshape(b * s, RET_WIDTH), rg, xp, *tail)
    keep = min(N_STEPS * max(DILATIONS), s)
    window = lambda a: a.reshape(b, ATTN_HEADS, ATTN_HEAD_DIM, s).transpose(0, 3, 1, 2)[:, s - keep:]
    k_win, v_win = window(kt), window(vt)

    xs = x_sample.reshape(bs, D_MODEL)
    aq, ak, av, rq, rk, rv, rg = _inproj(xs, norm1_g[0], w_in_b)
    position_minor = lambda c: c.transpose(0, 2, 3, 1)
    attn = _sample_attention(aq, ak, av, position_minor(cache_k_win[0]), position_minor(cache_v_win[0]))
    ret_o, s_new = _ret_sample(rq, rk, rv, state_ret[0], gam_tile)
    y_sample = _channel_and_norm(attn, ret_o, rg, xs, *tail)

    return (y_prompt.reshape(b, s, D_MODEL), y_sample.reshape(bs, 1, D_MODEL),
            k_win[None], v_win[None], s_fin[None],
            ak.reshape(1, bs, 1, ATTN_HEADS, ATTN_HEAD_DIM), av.reshape(1, bs, 1, ATTN_HEADS, ATTN_HEAD_DIM),
            s_new[None])
```

```python
import functools

import numpy as np
import jax
import jax.numpy as jnp
from jax import lax
from jax.experimental import pallas as pl
from jax.experimental.pallas import tpu as pltpu
from jax.experimental.pallas import tpu_sc as plsc

F32 = jnp.float32
MXU_DTYPE = jnp.bfloat16

D_MODEL = 1024
ATTN_WIDTH = 512
RET_WIDTH = 512
ATTN_HEAD_DIM = 64
ATTN_HEADS = 8
RET_HEADS = 4
RET_HEAD_DIM = 128
DILATIONS = (1, 4, 16)
N_STEPS = 128
ATTN_BLOCK = 128
ATTN_PAD = N_STEPS * max(DILATIONS)
RET_CHUNK = 128
PEER_HEADS = 8
PEER_N_KEYS = 128
PEER_TOPK = 16
PEER_HALF = 128
PEER_SLOTS = PEER_HEADS * PEER_TOPK
NORM_EPS = 1e-6
GN_EPS = 1e-5
IN_SPLITS = 7
RET_K_SCALE = RET_HEAD_DIM ** -0.5
ATTN_SCALE = ATTN_HEAD_DIM ** -0.5
LANES = 128
VMEM_LIMIT = 48 * 1024 * 1024

PEER_TOKENS_PER_STEP = 128
PEER_CHUNK = 8
PEER_SLAB = 2 * D_MODEL // LANES
PEER_PITCH = PEER_SLAB + 1
PEER_ISSUE_UNROLL = 8
SC_TOKENS = 9984
SC_ROWS = 16


def _params(*sem):
    return pltpu.CompilerParams(dimension_semantics=sem, vmem_limit_bytes=VMEM_LIMIT)


def _rms(x, g):
    r = lax.rsqrt(jnp.mean(x * x, axis=-1, keepdims=True) + NORM_EPS)
    return x * r * g


def _inproj_kernel(x_ref, g_ref, w_ref, *refs):
    out_refs = refs[-IN_SPLITS:] if len(refs) == IN_SPLITS else refs[1:1 + IN_SPLITS]
    xn = _rms(x_ref[...], g_ref[...]).astype(MXU_DTYPE)
    for i, o_ref in enumerate(out_refs):
        y = jnp.dot(xn, w_ref[:, i * ATTN_WIDTH:(i + 1) * ATTN_WIDTH], preferred_element_type=F32)
        if i == 4:
            y = y * RET_K_SCALE
        o_ref[...] = y
    if len(refs) > IN_SPLITS:
        wkv_t_ref, kt_ref, vt_ref = refs[0], refs[-2], refs[-1]
        nt = (((1,), (1,)), ((), ()))
        kv_t = lax.dot_general(wkv_t_ref[...], xn, nt, preferred_element_type=F32)
        kt_ref[...] = kv_t[:ATTN_WIDTH]
        vt_ref[...] = kv_t[ATTN_WIDTH:]


def _inproj(x, g, w, seq_len=None):
    n = x.shape[0]
    tm = min(256, n)
    out = jax.ShapeDtypeStruct((n, ATTN_WIDTH), F32)
    in_specs = [pl.BlockSpec((tm, D_MODEL), lambda i: (i, 0)),
                pl.BlockSpec((1, D_MODEL), lambda i: (0, 0)),
                pl.BlockSpec((D_MODEL, IN_SPLITS * ATTN_WIDTH), lambda i: (0, 0))]
    out_specs = [pl.BlockSpec((tm, ATTN_WIDTH), lambda i: (i, 0))] * IN_SPLITS
    out_shape = [out] * IN_SPLITS
    args = [x, g.reshape(1, D_MODEL), w]
    if seq_len is not None:
        per_row = seq_len // tm
        in_specs.append(pl.BlockSpec((2 * ATTN_WIDTH, D_MODEL), lambda i: (0, 0)))
        args.append(w[:, ATTN_WIDTH:3 * ATTN_WIDTH].T)
        out_specs += [pl.BlockSpec((None, ATTN_WIDTH, tm), lambda i: (i // per_row, 0, i % per_row))] * 2
        out_shape += [jax.ShapeDtypeStruct((n // seq_len, ATTN_WIDTH, seq_len), F32)] * 2
    return pl.pallas_call(
        _inproj_kernel,
        grid=(n // tm,),
        in_specs=in_specs,
        out_specs=out_specs,
        out_shape=out_shape,
        compiler_params=_params("parallel"),
    )(*args)


def _merge_groups(outs, lses):
    m = functools.reduce(jnp.maximum, lses)
    es = [jnp.exp(l - m) for l in lses]
    den = functools.reduce(jnp.add, es)
    return functools.reduce(jnp.add, [(e / den) * o for e, o in zip(es, outs)])


def _prompt_attn_kernel(s_len, q_ref, k_ref, v_ref, slope_ref, o_ref, og_ref, lg_ref):
    lane = lax.broadcasted_iota(jnp.int32, (ATTN_BLOCK, LANES), 1)
    first = lane < ATTN_HEAD_DIM
    qi = lax.broadcasted_iota(jnp.int32, (ATTN_BLOCK, 2 * ATTN_BLOCK), 0)
    kj = lax.broadcasted_iota(jnp.int32, (ATTN_BLOCK, 2 * ATTN_BLOCK), 1)
    back = qi + ATTN_BLOCK - kj
    in_band = (back >= 0) & (back <= N_STEPS)
    nt = (((1,), (1,)), ((), ()))
    ntile = s_len // ATTN_BLOCK
    for g, dil in enumerate(DILATIONS):
        dist = (back * dil).astype(F32)
        bias = [jnp.where(in_band, -slope_ref[hh, 0:1, 0:1] * dist, -jnp.inf) for hh in range(2)]

        def tile(t, carry, g=g, dil=dil, bias=bias):
            r, n = t % dil, t // dil
            qrow = r + n * (dil * ATTN_BLOCK)
            krow = ATTN_PAD + qrow - dil * ATTN_BLOCK
            q = q_ref[pl.ds(qrow, ATTN_BLOCK, stride=dil), :]
            kb = k_ref[pl.ds(krow, 2 * ATTN_BLOCK, stride=dil), :].astype(MXU_DTYPE)
            vb = v_ref[pl.ds(krow, 2 * ATTN_BLOCK, stride=dil), :].astype(MXU_DTYPE)
            exists = kj + n * ATTN_BLOCK >= ATTN_BLOCK
            outs, lses = [], []
            for hh in range(2):
                qm = jnp.where(first if hh == 0 else ~first, q, 0.0).astype(MXU_DTYPE)
                sc = lax.dot_general(qm, kb, nt, preferred_element_type=F32) * ATTN_SCALE + bias[hh]
                sc = jnp.where(exists, sc, -jnp.inf)
                m = jnp.max(sc, -1, keepdims=True)
                p = jnp.exp(sc - m)
                den = jnp.sum(p, -1, keepdims=True)
                outs.append(jnp.dot((p / den).astype(MXU_DTYPE), vb, preferred_element_type=F32))
                lses.append(jnp.broadcast_to(m + jnp.log(den), (ATTN_BLOCK, LANES)))
            og_ref[g, pl.ds(qrow, ATTN_BLOCK, stride=dil), :] = jnp.where(first, outs[0], outs[1])
            lg_ref[g, pl.ds(qrow, ATTN_BLOCK, stride=dil), :] = jnp.where(first, lses[0], lses[1])
            return carry

        lax.fori_loop(0, ntile, tile, 0, unroll=2)

    def merge(i, carry):
        rows = pl.ds(pl.multiple_of(i * ATTN_BLOCK, ATTN_BLOCK), ATTN_BLOCK)
        groups = range(len(DILATIONS))
        o_ref[rows, :] = _merge_groups([og_ref[g, rows, :] for g in groups], [lg_ref[g, rows, :] for g in groups])
        return carry

    lax.fori_loop(0, ntile, merge, 0)


def _prompt_attention(q, k, v, slopes_tile):
    b, s, _ = q.shape
    pad = lambda a: jnp.pad(a, ((0, 0), (ATTN_PAD, 0), (0, 0)))
    q_spec = pl.BlockSpec((None, s, LANES), lambda i, h: (i, 0, h))
    kv_spec = pl.BlockSpec((None, ATTN_PAD + s, LANES), lambda i, h: (i, 0, h))
    return pl.pallas_call(
        functools.partial(_prompt_attn_kernel, s),
        grid=(b, ATTN_WIDTH // LANES),
        in_specs=[q_spec, kv_spec, kv_spec, pl.BlockSpec((None, 2, 8, LANES), lambda i, h: (h, 0, 0, 0))],
        out_specs=q_spec,
        out_shape=jax.ShapeDtypeStruct((b, s, ATTN_WIDTH), F32),
        scratch_shapes=[pltpu.VMEM((len(DILATIONS), s, LANES), F32), pltpu.VMEM((len(DILATIONS), s, LANES), F32)],
        compiler_params=_params("parallel", "parallel"),
    )(q, pad(k), pad(v), slopes_tile)


def _sample_attn_kernel(slopes, q_ref, kn_ref, vn_ref, kt_ref, vt_ref, o_ref):
    w = kt_ref.shape[-1]
    back_i = w - lax.broadcasted_iota(jnp.int32, (1, w), 1)
    back = back_i.astype(F32)
    valid = [(back_i % dil == 0) & (back_i <= N_STEPS * dil) for dil in DILATIONS]
    for hd in range(ATTN_HEADS):
        q = q_ref[hd]
        raw = jnp.sum(kt_ref[hd] * q, 0, keepdims=True) * ATTN_SCALE - slopes[hd] * back
        l0 = jnp.sum(q * kn_ref[hd], 0, keepdims=True) * ATTN_SCALE
        vt = vt_ref[hd]
        outs, lses = [], []
        for ok in valid:
            m = jnp.maximum(jnp.max(jnp.where(ok, raw, -jnp.inf), -1, keepdims=True), l0)
            p = jnp.where(ok, jnp.exp(raw - m), 0.0)
            p0 = jnp.exp(l0 - m)
            s = jnp.sum(p, -1, keepdims=True) + p0
            outs.append((jnp.sum(vt * p, -1, keepdims=True) + p0 * vn_ref[hd]) / s)
            lses.append(m + jnp.log(s))
        o_ref[hd] = _merge_groups(outs, lses)


def _sample_attention(q, kn, vn, cache_kt, cache_vt):
    b = q.shape[0]
    assert cache_kt.shape[-1] >= N_STEPS * max(DILATIONS)
    cols = lambda a: a.reshape(b, ATTN_HEADS, ATTN_HEAD_DIM, 1)
    col_spec = pl.BlockSpec((None, ATTN_HEADS, ATTN_HEAD_DIM, 1), lambda i: (i, 0, 0, 0))
    win_spec = pl.BlockSpec((None,) + cache_kt.shape[1:], lambda i: (i, 0, 0, 0))
    out = pl.pallas_call(
        functools.partial(_sample_attn_kernel, [float(v) for v in _alibi_slopes()]),
        grid=(b,),
        in_specs=[col_spec, col_spec, col_spec, win_spec, win_spec],
        out_specs=col_spec,
        out_shape=jax.ShapeDtypeStruct((b, ATTN_HEADS, ATTN_HEAD_DIM, 1), F32),
        compiler_params=_params("parallel"),
    )(cols(q), cols(kn), cols(vn), cache_kt, cache_vt)
    return out.reshape(b, ATTN_WIDTH)


def _ret_prompt_kernel(nchunk, q_ref, k_ref, v_ref, dmat_ref, qdec_ref, kdec_ref, cdec_ref, o_ref, s_ref):
    s_ref[...] = jnp.zeros_like(s_ref)
    nt = (((1,), (1,)), ((), ()))
    tn = (((0,), (0,)), ((), ()))

    def chunk(c, carry):
        row = pl.multiple_of(c * RET_CHUNK, RET_CHUNK)
        q = q_ref[pl.ds(row, RET_CHUNK), :]
        k = k_ref[pl.ds(row, RET_CHUNK), :]
        v = v_ref[pl.ds(row, RET_CHUNK), :].astype(MXU_DTYPE)
        qb = q.astype(MXU_DTYPE)
        state = s_ref[...]
        inner = lax.dot_general(qb, k.astype(MXU_DTYPE), nt, preferred_element_type=F32) * dmat_ref[...]
        o_in = jnp.dot(inner.astype(MXU_DTYPE), v, preferred_element_type=F32)
        o_x = jnp.dot(qb, state.astype(MXU_DTYPE), preferred_element_type=F32) * qdec_ref[...]
        o_ref[pl.ds(row, RET_CHUNK), :] = o_in + o_x
        kd = (k * kdec_ref[...]).astype(MXU_DTYPE)
        s_ref[...] = cdec_ref[0:1, :] * state + lax.dot_general(kd, v, tn, preferred_element_type=F32)
        return carry

    lax.fori_loop(0, nchunk, chunk, 0)


def _ret_prompt(q, k, v, consts):
    b, s, _ = q.shape
    seq = pl.BlockSpec((None, s, RET_HEAD_DIM), lambda i, h: (i, 0, h))
    per_head = lambda rows: pl.BlockSpec((None, rows, RET_HEAD_DIM), lambda i, h: (h, 0, 0))
    return pl.pallas_call(
        functools.partial(_ret_prompt_kernel, s // RET_CHUNK),
        grid=(b, RET_HEADS),
        in_specs=[seq, seq, seq, per_head(RET_CHUNK), per_head(RET_CHUNK), per_head(RET_CHUNK), per_head(8)],
        out_specs=[seq, pl.BlockSpec((None, None, RET_HEAD_DIM, RET_HEAD_DIM), lambda i, h: (i, h, 0, 0))],
        out_shape=[jax.ShapeDtypeStruct((b, s, RET_WIDTH), F32),
                   jax.ShapeDtypeStruct((b, RET_HEADS, RET_HEAD_DIM, RET_HEAD_DIM), F32)],
        compiler_params=_params("parallel", "parallel"),
    )(q, k, v, *consts)


def _ret_sample_kernel(qc_ref, kc_ref, v_ref, s_ref, gam_ref, o_ref, sn_ref):
    for h in range(RET_HEADS):
        qc = qc_ref[h]
        kc = kc_ref[h]
        v = v_ref[:, h * RET_HEAD_DIM:(h + 1) * RET_HEAD_DIM]
        gam = gam_ref[h, 0:1, :]
        state = s_ref[h]
        inner = jnp.sum(qc * kc, 0, keepdims=True)
        o_x = jnp.sum(qc * state, 0, keepdims=True)
        o_ref[:, h * RET_HEAD_DIM:(h + 1) * RET_HEAD_DIM] = inner * v + gam * o_x
        sn_ref[h] = gam * state + kc * v


def _ret_sample(q, k, v, state, gam_tile):
    b = q.shape[0]
    col = lambda a: a.reshape(b, RET_HEADS, RET_HEAD_DIM, 1)
    col_spec = pl.BlockSpec((None, RET_HEADS, RET_HEAD_DIM, 1), lambda i: (i, 0, 0, 0))
    row_spec = pl.BlockSpec((None, 1, RET_WIDTH), lambda i: (i, 0, 0))
    st_spec = pl.BlockSpec((None, RET_HEADS, RET_HEAD_DIM, RET_HEAD_DIM), lambda i: (i, 0, 0, 0))
    o, sn = pl.pallas_call(
        _ret_sample_kernel,
        grid=(b,),
        in_specs=[col_spec, col_spec, row_spec, st_spec,
                  pl.BlockSpec((RET_HEADS, 8, RET_HEAD_DIM), lambda i: (0, 0, 0))],
        out_specs=[row_spec, st_spec],
        out_shape=[jax.ShapeDtypeStruct((b, 1, RET_WIDTH), F32), jax.ShapeDtypeStruct(state.shape, F32)],
        compiler_params=_params("parallel"),
    )(col(q), col(k), v.reshape(b, 1, RET_WIDTH), state, gam_tile)
    return o.reshape(b, RET_WIDTH), sn


def _mixer_out_kernel(attn_ref, ret_ref, gate_ref, x_ref, gn_ref, wout_ref, n2_ref, wpq_ref, keys_ref,
                      h_ref, hn_ref, st_ref):
    gate = gate_ref[...]
    swish = gate * (1.0 / (1.0 + jnp.exp(-gate)))
    normed = []
    for hd in range(RET_HEADS):
        cols = slice(hd * RET_HEAD_DIM, (hd + 1) * RET_HEAD_DIM)
        r = ret_ref[:, cols]
        mu = jnp.mean(r, -1, keepdims=True)
        var = jnp.mean(jnp.square(r - mu), -1, keepdims=True)
        normed.append((r - mu) * lax.rsqrt(var + GN_EPS) * gn_ref[:, cols])
    ret_y = swish * jnp.concatenate(normed, -1)
    h = (x_ref[...]
         + jnp.dot(attn_ref[...].astype(MXU_DTYPE), wout_ref[0:ATTN_WIDTH, :], preferred_element_type=F32)
         + jnp.dot(ret_y.astype(MXU_DTYPE), wout_ref[ATTN_WIDTH:, :], preferred_element_type=F32))
    h_ref[...] = h
    hn = _rms(h, n2_ref[...])
    hn_ref[...] = hn
    hb = hn.astype(MXU_DTYPE)
    nt = (((1,), (1,)), ((), ()))
    for j in range(2 * PEER_HEADS):
        qj = jnp.dot(hb, wpq_ref[:, j * PEER_HALF:(j + 1) * PEER_HALF], preferred_element_type=F32)
        st_ref[j] = lax.dot_general(keys_ref[j], qj.astype(MXU_DTYPE), nt, preferred_element_type=F32)


def _mixer_out(attn, ret_o, gate, x, gn_g, w_out, n2_g, w_pq, keys):
    n = x.shape[0]
    tm = min(256, n)
    half = pl.BlockSpec((tm, ATTN_WIDTH), lambda i: (i, 0))
    full = pl.BlockSpec((tm, D_MODEL), lambda i: (i, 0))
    const = lambda shape: pl.BlockSpec(shape, lambda i: (0,) * len(shape))
    return pl.pallas_call(
        _mixer_out_kernel,
        grid=(n // tm,),
        in_specs=[half] * 3 + [full, const((1, RET_WIDTH)), const((D_MODEL, D_MODEL)), const((1, D_MODEL)),
                               const((D_MODEL, 2 * PEER_HEADS * PEER_HALF)),
                               const((2 * PEER_HEADS, PEER_N_KEYS, PEER_HALF))],
        out_specs=[full, full, pl.BlockSpec((2 * PEER_HEADS, PEER_N_KEYS, tm), lambda i: (0, 0, i))],
        out_shape=[jax.ShapeDtypeStruct((n, D_MODEL), F32), jax.ShapeDtypeStruct((n, D_MODEL), F32),
                   jax.ShapeDtypeStruct((2 * PEER_HEADS, PEER_N_KEYS, n), F32)],
        compiler_params=_params("parallel"),
    )(attn, ret_o, gate, x, gn_g.reshape(1, RET_WIDTH), w_out, n2_g.reshape(1, D_MODEL), w_pq, keys)


def _top_rows(s, k, payload=None):
    nrow = s.shape[0]
    ridx = lax.broadcasted_iota(jnp.int32, s.shape, 0)
    vals, picks = [], []
    for _ in range(k):
        m = jnp.max(s, 0, keepdims=True)
        am = jnp.min(jnp.where(s == m, ridx, nrow), 0, keepdims=True)
        hit = ridx == am
        vals.append(m)
        picks.append(am if payload is None else jnp.sum(jnp.where(hit, payload, 0), 0, keepdims=True))
        s = jnp.where(hit, -jnp.inf, s)
    return jnp.concatenate(vals, 0), jnp.concatenate(picks, 0)


def _product_grid(a0, a1, op):
    pieces = [op(a0[0:1], a1)]
    pieces += [op(a0[a:a + 1], a1[0:8]) for a in range(1, 8)]
    pieces.append(op(a0[8:16], a1[0:1]))
    return jnp.concatenate(pieces, 0)


def _topk_kernel(st_ref, eid_ref, g_ref):
    def head(hd, carry):
        sv0, si0 = _top_rows(st_ref[2 * hd], PEER_TOPK)
        sv1, si1 = _top_rows(st_ref[2 * hd + 1], PEER_TOPK)
        cand = _product_grid(sv0, sv1, jnp.add)
        cidx = _product_grid(si0, si1, lambda i, j: i * PEER_N_KEYS + j)
        best, eid = _top_rows(cand, PEER_TOPK, payload=cidx)
        e = jnp.exp(best - jnp.max(best, 0, keepdims=True))
        rows = pl.ds(pl.multiple_of(hd * PEER_TOPK, PEER_TOPK), PEER_TOPK)
        eid_ref[rows, :] = eid
        g_ref[rows, :] = e / jnp.sum(e, 0, keepdims=True)
        return carry

    lax.fori_loop(0, PEER_HEADS, head, 0)


def _topk(st):
    n = st.shape[-1]
    tn = LANES
    out_spec = pl.BlockSpec((PEER_SLOTS, tn), lambda i: (0, i))
    return pl.pallas_call(
        _topk_kernel,
        grid=(n // tn,),
        in_specs=[pl.BlockSpec((2 * PEER_HEADS, PEER_N_KEYS, tn), lambda i: (0, 0, i))],
        out_specs=[out_spec, out_spec],
        out_shape=[jax.ShapeDtypeStruct((PEER_SLOTS, n), jnp.int32), jax.ShapeDtypeStruct((PEER_SLOTS, n), F32)],
        compiler_params=_params("parallel"),
    )(st)


def _peer_kernel(first_step, eid_hbm, gt_ref, hn_ref, h_ref, nf_ref, uv_hbm, y_ref, eid_smem, buf, sems):
    step = pl.program_id(0) + first_step
    nchunk = PEER_TOKENS_PER_STEP // PEER_CHUNK
    nexp = PEER_CHUNK * PEER_SLOTS
    ids = pltpu.make_async_copy(eid_hbm.at[step], eid_smem, sems.at[2])
    ids.start()
    ids.wait()

    def start_copy(c, i, slot, k):
        e = eid_smem[c * nexp + i]
        src = uv_hbm.at[pl.ds(pl.multiple_of(e * PEER_SLAB, PEER_SLAB), PEER_SLAB), :]
        dst = buf.at[slot, pl.ds(i * PEER_PITCH, PEER_SLAB), :]
        pltpu.make_async_copy(src, dst, sems.at[slot]).start(priority=k % 2)

    def gather(c, slot):
        def group(j, carry):
            for k in range(PEER_ISSUE_UNROLL):
                start_copy(c, j * PEER_ISSUE_UNROLL + k, slot, k)
            return carry
        lax.fori_loop(0, nexp // PEER_ISSUE_UNROLL, group, 0)

    def wait(slot):
        span = pl.ds(0, nexp * PEER_SLAB)
        pltpu.make_async_copy(uv_hbm.at[span, :], buf.at[slot, span, :], sems.at[slot]).wait()

    def rows_of(slot, t, first_row):
        base = t * PEER_SLOTS * PEER_PITCH + first_row
        return jnp.concatenate(
            [jnp.concatenate([buf[slot, pl.ds(base + kg * 8 * PEER_PITCH + s, 8, stride=PEER_PITCH), :]
                              for s in range(D_MODEL // LANES)], axis=1)
             for kg in range(PEER_SLOTS // 8)], axis=0)

    tok_lane = lax.broadcasted_iota(jnp.int32, (PEER_SLOTS, PEER_TOKENS_PER_STEP), 1)
    gather(0, 0)

    def chunk(c, carry):
        slot = c % 2

        @pl.when(c + 1 < nchunk)
        def _():
            gather(c + 1, 1 - slot)

        wait(slot)
        for t in range(PEER_CHUNK):
            tok = c * PEER_CHUNK + t
            x = hn_ref[pl.ds(tok, 1), :]
            a = jax.nn.gelu(jnp.sum(rows_of(slot, t, 0) * x, -1, keepdims=True))
            gate = jnp.sum(jnp.where(tok_lane == tok, gt_ref[...], 0.0), -1, keepdims=True)
            out = jnp.sum((gate * a) * rows_of(slot, t, PEER_SLAB // 2), 0, keepdims=True)
            y_ref[pl.ds(tok, 1), :] = _rms(h_ref[pl.ds(tok, 1), :] + out, nf_ref[...])
        return carry

    lax.fori_loop(0, nchunk, chunk, 0)


def _expert_slabs(peer_u, peer_v):
    e = peer_u.shape[0]
    tiles = lambda a: a.reshape(e, D_MODEL // LANES, LANES)
    return jnp.concatenate([tiles(peer_u), tiles(peer_v)], axis=1).reshape(e * PEER_SLAB, LANES)


def _peer_sc_experts(eid_flat, g_flat, hn, uv_rows, n_sc):
    info = plsc.get_sparse_core_info()
    nw = info.num_cores * info.num_subcores
    nl = info.num_lanes
    assert SC_ROWS == nl and n_sc % nw == 0
    per_w = n_sc // nw
    ngroup = PEER_SLOTS // SC_ROWS
    nvec = D_MODEL // nl
    mesh = plsc.VectorSubcoreMesh(core_axis_name="c", subcore_axis_name="s")
    group_buf = pltpu.VMEM((SC_ROWS, 2 * D_MODEL), F32)

    @functools.partial(
        pl.kernel, mesh=mesh,
        out_type=jax.ShapeDtypeStruct((n_sc, D_MODEL), F32),
        scratch_types=[pltpu.VMEM((PEER_SLOTS,), jnp.int32), pltpu.VMEM((PEER_SLOTS,), F32),
                       pltpu.VMEM((D_MODEL,), F32), group_buf, group_buf,
                       pltpu.VMEM((D_MODEL,), F32), pltpu.VMEM((SC_ROWS,), F32),
                       pltpu.SemaphoreType.DMA, pltpu.SemaphoreType.DMA],
        compiler_params=pltpu.CompilerParams(needs_layout_passes=False),
    )
    def k(eid_hbm, g_hbm, hn_hbm, uv_hbm, out_hbm, idx_v, g_v, x_v, rows_a, rows_b, o_v, w_v, sem_a, sem_b):
        wid = lax.axis_index("s") * info.num_cores + lax.axis_index("c")
        lane = lax.iota(jnp.int32, nl)

        def group_copy(gi, rows, sem):
            ids = idx_v.at[pl.ds(pl.multiple_of(gi * SC_ROWS, SC_ROWS), SC_ROWS)]
            return pltpu.make_async_copy(uv_hbm.at[ids], rows, sem)

        def reduce_group(gi, rows):
            def dot_pair(rp, a_vec):
                r0, r1 = 2 * rp, 2 * rp + 1
                acc = [jnp.zeros((nl,), F32) for _ in range(4)]
                for c in range(nvec):
                    xc = x_v[pl.ds(c * nl, nl)]
                    acc[c % 2] = acc[c % 2] + rows[r0, pl.ds(c * nl, nl)] * xc
                    acc[2 + c % 2] = acc[2 + c % 2] + rows[r1, pl.ds(c * nl, nl)] * xc
                a_vec = jnp.where(lane == r0, jnp.sum(acc[0] + acc[1]), a_vec)
                return jnp.where(lane == r1, jnp.sum(acc[2] + acc[3]), a_vec)
            a = lax.fori_loop(0, SC_ROWS // 2, dot_pair, jnp.zeros((nl,), F32))
            z = 0.7978845608028654 * (a + 0.044715 * (a * a * a))
            tanh_z = 1.0 - 2.0 / (1.0 + jnp.exp(2.0 * z))
            gates = g_v[pl.ds(pl.multiple_of(gi * SC_ROWS, SC_ROWS), SC_ROWS)]
            w_v[...] = gates * (0.5 * a * (1.0 + tanh_z))

            @pl.loop(0, SC_ROWS // 2)
            def _(rp):
                r0, r1 = 2 * rp, 2 * rp + 1
                w0 = plsc.load_gather(w_v, [jnp.full((nl,), r0, jnp.int32)])
                w1 = plsc.load_gather(w_v, [jnp.full((nl,), r1, jnp.int32)])
                for c in range(nvec):
                    cols = pl.ds(D_MODEL + c * nl, nl)
                    plsc.addupdate(o_v.at[pl.ds(c * nl, nl)], w0 * rows[r0, cols] + w1 * rows[r1, cols])

        @pl.loop(0, per_w)
        def _(ti):
            tok = wid * per_w + ti
            base = pl.multiple_of(tok * PEER_SLOTS, PEER_SLOTS)
            pltpu.sync_copy(eid_hbm.at[pl.ds(base, PEER_SLOTS)], idx_v)
            group_copy(0, rows_a, sem_a).start()
            pltpu.sync_copy(g_hbm.at[pl.ds(base, PEER_SLOTS)], g_v)
            pltpu.sync_copy(hn_hbm.at[tok], x_v)
            for c in range(nvec):
                o_v[pl.ds(c * nl, nl)] = jnp.zeros((nl,), F32)

            @pl.loop(0, ngroup // 2)
            def _(j):
                group_copy(2 * j + 1, rows_b, sem_b).start()
                group_copy(2 * j, rows_a, sem_a).wait()
                reduce_group(2 * j, rows_a)

                @pl.when(j + 1 < ngroup // 2)
                def _():
                    group_copy(2 * j + 2, rows_a, sem_a).start()

                group_copy(2 * j + 1, rows_b, sem_b).wait()
                reduce_group(2 * j + 1, rows_b)

            pltpu.sync_copy(o_v, out_hbm.at[tok])

    return k(eid_flat, g_flat, hn, uv_rows)


def _residual_norm_kernel(h_ref, p_ref, nf_ref, y_ref):
    y_ref[...] = _rms(h_ref[...] + p_ref[...], nf_ref[...])


def _peer(eid_t, g_t, hn, h, nf_g, uv):
    n = hn.shape[0]
    tp = PEER_TOKENS_PER_STEP
    n_sc = SC_TOKENS if n >= 2 * SC_TOKENS else 0
    eid_rows = eid_t.T
    eid_tok = eid_rows.reshape(n // tp, tp * PEER_SLOTS)
    rows = PEER_CHUNK * PEER_SLOTS * PEER_PITCH
    first = n_sc // tp
    tok = pl.BlockSpec((tp, D_MODEL), lambda i: (i + first, 0))
    y_tc = pl.pallas_call(
        functools.partial(_peer_kernel, first),
        grid=((n - n_sc) // tp,),
        in_specs=[pl.BlockSpec(memory_space=pl.ANY),
                  pl.BlockSpec((PEER_SLOTS, tp), lambda i: (0, i + first)),
                  tok, tok,
                  pl.BlockSpec((1, D_MODEL), lambda i: (0, 0)),
                  pl.BlockSpec(memory_space=pl.ANY)],
        out_specs=pl.BlockSpec((tp, D_MODEL), lambda i: (i, 0)),
        out_shape=jax.ShapeDtypeStruct((n - n_sc, D_MODEL), F32),
        scratch_shapes=[pltpu.SMEM((tp * PEER_SLOTS,), jnp.int32),
                        pltpu.VMEM((2, rows, LANES), F32),
                        pltpu.SemaphoreType.DMA((3,))],
        compiler_params=_params("arbitrary"),
    )(eid_tok, g_t, hn, h, nf_g.reshape(1, D_MODEL), uv)
    if n_sc == 0:
        return y_tc
    p_sc = _peer_sc_experts(eid_rows.reshape(-1), g_t.T.reshape(-1), hn,
                            uv.reshape(-1, 2 * D_MODEL), n_sc)
    tm = 256
    blk = pl.BlockSpec((tm, D_MODEL), lambda i: (i, 0))
    y_sc = pl.pallas_call(
        _residual_norm_kernel,
        grid=(n_sc // tm,),
        in_specs=[blk, blk, pl.BlockSpec((1, D_MODEL), lambda i: (0, 0))],
        out_specs=blk,
        out_shape=jax.ShapeDtypeStruct((n_sc, D_MODEL), F32),
        compiler_params=_params("parallel"),
    )(h, p_sc, nf_g.reshape(1, D_MODEL))
    return jnp.concatenate([y_sc, y_tc], axis=0)


def _alibi_slopes():
    return (2.0 ** (-8.0 * (np.arange(ATTN_HEADS) + 1) / ATTN_HEADS)).astype(np.float32)


def _ret_consts():
    log_gamma = np.log(1.0 - 2.0 ** (-5.0 - np.arange(RET_HEADS))).astype(np.float32)
    pos = np.arange(RET_CHUNK, dtype=np.float32)
    rel = pos[:, None] - pos[None, :]
    dmat = np.where(rel[None] >= 0, np.exp(np.maximum(rel, 0.0)[None] * log_gamma[:, None, None]), 0.0)
    lanes = lambda col: np.broadcast_to(col[:, :, None], (RET_HEADS, col.shape[1], RET_HEAD_DIM))
    qdec = lanes(np.exp((pos[None, :] + 1.0) * log_gamma[:, None]))
    kdec = lanes(np.exp((RET_CHUNK - 1.0 - pos)[None, :] * log_gamma[:, None]))
    cdec = lanes(np.broadcast_to(np.exp(RET_CHUNK * log_gamma)[:, None], (RET_HEADS, 8)))
    gam = lanes(np.broadcast_to(np.exp(log_gamma)[:, None], (RET_HEADS, 8)))
    as_f32 = lambda a: jnp.asarray(np.ascontiguousarray(a, dtype=np.float32))
    return tuple(as_f32(a) for a in (dmat, qdec, kdec, cdec)), as_f32(gam)


def _channel_and_norm(attn, ret_o, gate, x, gn_g, w_out, n2_g, w_pq, keys, uv, nf_g):
    h, hn, st = _mixer_out(attn, ret_o, gate, x, gn_g, w_out, n2_g, w_pq, keys)
    eid_t, g_t = _topk(st)
    return _peer(eid_t, g_t, hn, h, nf_g, uv)


def kernel(x_prompt, x_sample, cache_k_win, cache_v_win, state_ret, norm1_g, w_in, ret_gn_g, w_out, norm2_g, w_pq, peer_sub_keys, peer_u, peer_v, norm_f_g):
    depth = w_in.shape[0]
    assert depth == 1, "single-layer stack"
    b, s, _ = x_prompt.shape
    bs = x_sample.shape[0]
    assert x_sample.shape[1] == 1 and s % (ATTN_BLOCK * max(DILATIONS)) == 0
    assert cache_k_win.shape[2] % (N_STEPS * max(DILATIONS)) == 0

    slopes = _alibi_slopes()
    slopes_tile = jnp.asarray(np.broadcast_to(slopes.reshape(ATTN_HEADS // 2, 2, 1, 1), (ATTN_HEADS // 2, 2, 8, LANES)).copy())
    ret_consts, gam_tile = _ret_consts()

    w_in_b = w_in[0].astype(MXU_DTYPE)
    w_out_b = w_out[0].astype(MXU_DTYPE)
    w_pq_b = w_pq[0].astype(MXU_DTYPE)
    keys_b = peer_sub_keys[0].reshape(2 * PEER_HEADS, PEER_N_KEYS, PEER_HALF).astype(MXU_DTYPE)
    uv = _expert_slabs(peer_u[0], peer_v[0])
    tail = (ret_gn_g[0], w_out_b, norm2_g[0], w_pq_b, keys_b, uv, norm_f_g)

    xp = x_prompt.reshape(b * s, D_MODEL)
    aq, ak, av, rq, rk, rv, rg, kt, vt = _inproj(xp, norm1_g[0], w_in_b, seq_len=s)
    seq = lambda a: a.reshape(b, s, ATTN_WIDTH)
    attn = _prompt_attention(seq(aq), seq(ak), seq(av), slopes_tile)
    ret_o, s_fin = _ret_prompt(seq(rq), seq(rk), seq(rv), ret_consts)
    y_prompt = _channel_and_norm(attn.reshape(b * s, ATTN_WIDTH), ret_o.reshape(b * s, RET_WIDTH), rg, xp, *tail)
    keep = min(N_STEPS * max(DILATIONS), s)
    window = lambda a: a.reshape(b, ATTN_HEADS, ATTN_HEAD_DIM, s).transpose(0, 3, 1, 2)[:, s - keep:]
    k_win, v_win = window(kt), window(vt)

    xs = x_sample.reshape(bs, D_MODEL)
    aq, ak, av, rq, rk, rv, rg = _inproj(xs, norm1_g[0], w_in_b)
    position_minor = lambda c: c.transpose(0, 2, 3, 1)
    attn = _sample_attention(aq, ak, av, position_minor(cache_k_win[0]), position_minor(cache_v_win[0]))
    ret_o, s_new = _ret_sample(rq, rk, rv, state_ret[0], gam_tile)
    y_sample = _channel_and_norm(attn, ret_o, rg, xs, *tail)

    return (y_prompt.reshape(b, s, D_MODEL), y_sample.reshape(bs, 1, D_MODEL),
            k_win[None], v_win[None], s_fin[None],
            ak.reshape(1, bs, 1, ATTN_HEADS, ATTN_HEAD_DIM), av.reshape(1, bs, 1, ATTN_HEADS, ATTN_HEAD_DIM),
            s_new[None])
```

```python
import functools

import numpy as np
import jax
import jax.numpy as jnp
from jax import lax
from jax.experimental import pallas as pl
from jax.experimental.pallas import tpu as pltpu
from jax.experimental.pallas import tpu_sc as plsc

F32 = jnp.float32
MXU_DTYPE = jnp.bfloat16

D_MODEL = 1024
ATTN_WIDTH = 512
RET_WIDTH = 512
ATTN_HEAD_DIM = 64
ATTN_HEADS = 8
RET_HEADS = 4
RET_HEAD_DIM = 128
DILATIONS = (1, 4, 16)
N_STEPS = 128
ATTN_BLOCK = 128
ATTN_PAD = N_STEPS * max(DILATIONS)
RET_CHUNK = 128
PEER_HEADS = 8
PEER_N_KEYS = 128
PEER_TOPK = 16
PEER_HALF = 128
PEER_SLOTS = PEER_HEADS * PEER_TOPK
NORM_EPS = 1e-6
GN_EPS = 1e-5
IN_SPLITS = 7
RET_K_SCALE = RET_HEAD_DIM ** -0.5
ATTN_SCALE = ATTN_HEAD_DIM ** -0.5
LANES = 128
VMEM_LIMIT = 48 * 1024 * 1024

PEER_TOKENS_PER_STEP = 128
PEER_CHUNK = 8
PEER_SLAB = 2 * D_MODEL // LANES
PEER_PITCH = PEER_SLAB + 1
PEER_ISSUE_UNROLL = 8
SC_TOKENS = 14592
SC_ROWS = 16


def _params(*sem):
    return pltpu.CompilerParams(dimension_semantics=sem, vmem_limit_bytes=VMEM_LIMIT)


def _rms(x, g):
    r = lax.rsqrt(jnp.mean(x * x, axis=-1, keepdims=True) + NORM_EPS)
    return x * r * g


def _inproj_kernel(x_ref, g_ref, w_ref, *refs):
    out_refs = refs[-IN_SPLITS:] if len(refs) == IN_SPLITS else refs[1:1 + IN_SPLITS]
    xn = _rms(x_ref[...], g_ref[...]).astype(MXU_DTYPE)
    for i, o_ref in enumerate(out_refs):
        y = jnp.dot(xn, w_ref[:, i * ATTN_WIDTH:(i + 1) * ATTN_WIDTH], preferred_element_type=F32)
        if i == 4:
            y = y * RET_K_SCALE
        o_ref[...] = y
    if len(refs) > IN_SPLITS:
        wkv_t_ref, kt_ref, vt_ref = refs[0], refs[-2], refs[-1]
        nt = (((1,), (1,)), ((), ()))
        kv_t = lax.dot_general(wkv_t_ref[...], xn, nt, preferred_element_type=F32)
        kt_ref[...] = kv_t[:ATTN_WIDTH]
        vt_ref[...] = kv_t[ATTN_WIDTH:]


def _inproj(x, g, w, seq_len=None):
    n = x.shape[0]
    tm = min(256, n)
    out = jax.ShapeDtypeStruct((n, ATTN_WIDTH), F32)
    in_specs = [pl.BlockSpec((tm, D_MODEL), lambda i: (i, 0)),
                pl.BlockSpec((1, D_MODEL), lambda i: (0, 0)),
                pl.BlockSpec((D_MODEL, IN_SPLITS * ATTN_WIDTH), lambda i: (0, 0))]
    out_specs = [pl.BlockSpec((tm, ATTN_WIDTH), lambda i: (i, 0))] * IN_SPLITS
    out_shape = [out] * IN_SPLITS
    args = [x, g.reshape(1, D_MODEL), w]
    if seq_len is not None:
        per_row = seq_len // tm
        in_specs.append(pl.BlockSpec((2 * ATTN_WIDTH, D_MODEL), lambda i: (0, 0)))
        args.append(w[:, ATTN_WIDTH:3 * ATTN_WIDTH].T)
        out_specs += [pl.BlockSpec((None, ATTN_WIDTH, tm), lambda i: (i // per_row, 0, i % per_row))] * 2
        out_shape += [jax.ShapeDtypeStruct((n // seq_len, ATTN_WIDTH, seq_len), F32)] * 2
    return pl.pallas_call(
        _inproj_kernel,
        grid=(n // tm,),
        in_specs=in_specs,
        out_specs=out_specs,
        out_shape=out_shape,
        compiler_params=_params("parallel"),
    )(*args)


def _merge_groups(outs, lses):
    m = functools.reduce(jnp.maximum, lses)
    es = [jnp.exp(l - m) for l in lses]
    den = functools.reduce(jnp.add, es)
    return functools.reduce(jnp.add, [(e / den) * o for e, o in zip(es, outs)])


def _prompt_attn_kernel(s_len, q_ref, k_ref, v_ref, slope_ref, o_ref, og_ref, lg_ref):
    lane = lax.broadcasted_iota(jnp.int32, (ATTN_BLOCK, LANES), 1)
    first = lane < ATTN_HEAD_DIM
    qi = lax.broadcasted_iota(jnp.int32, (ATTN_BLOCK, 2 * ATTN_BLOCK), 0)
    kj = lax.broadcasted_iota(jnp.int32, (ATTN_BLOCK, 2 * ATTN_BLOCK), 1)
    back = qi + ATTN_BLOCK - kj
    in_band = (back >= 0) & (back <= N_STEPS)
    nt = (((1,), (1,)), ((), ()))
    ntile = s_len // ATTN_BLOCK
    for g, dil in enumerate(DILATIONS):
        dist = (back * dil).astype(F32)
        bias = [jnp.where(in_band, -slope_ref[hh, 0:1, 0:1] * dist, -jnp.inf) for hh in range(2)]

        def tile(t, carry, g=g, dil=dil, bias=bias):
            r, n = t % dil, t // dil
            qrow = r + n * (dil * ATTN_BLOCK)
            krow = ATTN_PAD + qrow - dil * ATTN_BLOCK
            q = q_ref[pl.ds(qrow, ATTN_BLOCK, stride=dil), :]
            kb = k_ref[pl.ds(krow, 2 * ATTN_BLOCK, stride=dil), :].astype(MXU_DTYPE)
            vb = v_ref[pl.ds(krow, 2 * ATTN_BLOCK, stride=dil), :].astype(MXU_DTYPE)
            exists = kj + n * ATTN_BLOCK >= ATTN_BLOCK
            outs, lses = [], []
            for hh in range(2):
                qm = jnp.where(first if hh == 0 else ~first, q, 0.0).astype(MXU_DTYPE)
                sc = lax.dot_general(qm, kb, nt, preferred_element_type=F32) * ATTN_SCALE + bias[hh]
                sc = jnp.where(exists, sc, -jnp.inf)
                m = jnp.max(sc, -1, keepdims=True)
                p = jnp.exp(sc - m)
                den = jnp.sum(p, -1, keepdims=True)
                outs.append(jnp.dot((p / den).astype(MXU_DTYPE), vb, preferred_element_type=F32))
                lses.append(jnp.broadcast_to(m + jnp.log(den), (ATTN_BLOCK, LANES)))
            og_ref[g, pl.ds(qrow, ATTN_BLOCK, stride=dil), :] = jnp.where(first, outs[0], outs[1])
            lg_ref[g, pl.ds(qrow, ATTN_BLOCK, stride=dil), :] = jnp.where(first, lses[0], lses[1])
            return carry

        lax.fori_loop(0, ntile, tile, 0, unroll=2)

    def merge(i, carry):
        rows = pl.ds(pl.multiple_of(i * ATTN_BLOCK, ATTN_BLOCK), ATTN_BLOCK)
        groups = range(len(DILATIONS))
        o_ref[rows, :] = _merge_groups([og_ref[g, rows, :] for g in groups], [lg_ref[g, rows, :] for g in groups])
        return carry

    lax.fori_loop(0, ntile, merge, 0)


def _prompt_attention(q, k, v, slopes_tile):
    b, s, _ = q.shape
    pad = lambda a: jnp.pad(a, ((0, 0), (ATTN_PAD, 0), (0, 0)))
    q_spec = pl.BlockSpec((None, s, LANES), lambda i, h: (i, 0, h))
    kv_spec = pl.BlockSpec((None, ATTN_PAD + s, LANES), lambda i, h: (i, 0, h))
    return pl.pallas_call(
        functools.partial(_prompt_attn_kernel, s),
        grid=(b, ATTN_WIDTH // LANES),
        in_specs=[q_spec, kv_spec, kv_spec, pl.BlockSpec((None, 2, 8, LANES), lambda i, h: (h, 0, 0, 0))],
        out_specs=q_spec,
        out_shape=jax.ShapeDtypeStruct((b, s, ATTN_WIDTH), F32),
        scratch_shapes=[pltpu.VMEM((len(DILATIONS), s, LANES), F32), pltpu.VMEM((len(DILATIONS), s, LANES), F32)],
        compiler_params=_params("parallel", "parallel"),
    )(q, pad(k), pad(v), slopes_tile)


def _sample_attn_kernel(slopes, q_ref, kn_ref, vn_ref, kt_ref, vt_ref, o_ref):
    w = kt_ref.shape[-1]
    back_i = w - lax.broadcasted_iota(jnp.int32, (1, w), 1)
    back = back_i.astype(F32)
    valid = [(back_i % dil == 0) & (back_i <= N_STEPS * dil) for dil in DILATIONS]
    for hd in range(ATTN_HEADS):
        q = q_ref[hd]
        raw = jnp.sum(kt_ref[hd] * q, 0, keepdims=True) * ATTN_SCALE - slopes[hd] * back
        l0 = jnp.sum(q * kn_ref[hd], 0, keepdims=True) * ATTN_SCALE
        vt = vt_ref[hd]
        outs, lses = [], []
        for ok in valid:
            m = jnp.maximum(jnp.max(jnp.where(ok, raw, -jnp.inf), -1, keepdims=True), l0)
            p = jnp.where(ok, jnp.exp(raw - m), 0.0)
            p0 = jnp.exp(l0 - m)
            s = jnp.sum(p, -1, keepdims=True) + p0
            outs.append((jnp.sum(vt * p, -1, keepdims=True) + p0 * vn_ref[hd]) / s)
            lses.append(m + jnp.log(s))
        o_ref[hd] = _merge_groups(outs, lses)


def _sample_attention(q, kn, vn, cache_kt, cache_vt):
    b = q.shape[0]
    assert cache_kt.shape[-1] >= N_STEPS * max(DILATIONS)
    cols = lambda a: a.reshape(b, ATTN_HEADS, ATTN_HEAD_DIM, 1)
    col_spec = pl.BlockSpec((None, ATTN_HEADS, ATTN_HEAD_DIM, 1), lambda i: (i, 0, 0, 0))
    win_spec = pl.BlockSpec((None,) + cache_kt.shape[1:], lambda i: (i, 0, 0, 0))
    out = pl.pallas_call(
        functools.partial(_sample_attn_kernel, [float(v) for v in _alibi_slopes()]),
        grid=(b,),
        in_specs=[col_spec, col_spec, col_spec, win_spec, win_spec],
        out_specs=col_spec,
        out_shape=jax.ShapeDtypeStruct((b, ATTN_HEADS, ATTN_HEAD_DIM, 1), F32),
        compiler_params=_params("parallel"),
    )(cols(q), cols(kn), cols(vn), cache_kt, cache_vt)
    return out.reshape(b, ATTN_WIDTH)


def _ret_prompt_kernel(nchunk, q_ref, k_ref, v_ref, dmat_ref, qdec_ref, kdec_ref, cdec_ref, o_ref, s_ref):
    s_ref[...] = jnp.zeros_like(s_ref)
    nt = (((1,), (1,)), ((), ()))
    tn = (((0,), (0,)), ((), ()))

    def chunk(c, carry):
        row = pl.multiple_of(c * RET_CHUNK, RET_CHUNK)
        q = q_ref[pl.ds(row, RET_CHUNK), :]
        k = k_ref[pl.ds(row, RET_CHUNK), :]
        v = v_ref[pl.ds(row, RET_CHUNK), :].astype(MXU_DTYPE)
        qb = q.astype(MXU_DTYPE)
        state = s_ref[...]
        inner = lax.dot_general(qb, k.astype(MXU_DTYPE), nt, preferred_element_type=F32) * dmat_ref[...]
        o_in = jnp.dot(inner.astype(MXU_DTYPE), v, preferred_element_type=F32)
        o_x = jnp.dot(qb, state.astype(MXU_DTYPE), preferred_element_type=F32) * qdec_ref[...]
        o_ref[pl.ds(row, RET_CHUNK), :] = o_in + o_x
        kd = (k * kdec_ref[...]).astype(MXU_DTYPE)
        s_ref[...] = cdec_ref[0:1, :] * state + lax.dot_general(kd, v, tn, preferred_element_type=F32)
        return carry

    lax.fori_loop(0, nchunk, chunk, 0)


def _ret_prompt(q, k, v, consts):
    b, s, _ = q.shape
    seq = pl.BlockSpec((None, s, RET_HEAD_DIM), lambda i, h: (i, 0, h))
    per_head = lambda rows: pl.BlockSpec((None, rows, RET_HEAD_DIM), lambda i, h: (h, 0, 0))
    return pl.pallas_call(
        functools.partial(_ret_prompt_kernel, s // RET_CHUNK),
        grid=(b, RET_HEADS),
        in_specs=[seq, seq, seq, per_head(RET_CHUNK), per_head(RET_CHUNK), per_head(RET_CHUNK), per_head(8)],
        out_specs=[seq, pl.BlockSpec((None, None, RET_HEAD_DIM, RET_HEAD_DIM), lambda i, h: (i, h, 0, 0))],
        out_shape=[jax.ShapeDtypeStruct((b, s, RET_WIDTH), F32),
                   jax.ShapeDtypeStruct((b, RET_HEADS, RET_HEAD_DIM, RET_HEAD_DIM), F32)],
        compiler_params=_params("parallel", "parallel"),
    )(q, k, v, *consts)


def _ret_sample_kernel(qc_ref, kc_ref, v_ref, s_ref, gam_ref, o_ref, sn_ref):
    for h in range(RET_HEADS):
        qc = qc_ref[h]
        kc = kc_ref[h]
        v = v_ref[:, h * RET_HEAD_DIM:(h + 1) * RET_HEAD_DIM]
        gam = gam_ref[h, 0:1, :]
        state = s_ref[h]
        inner = jnp.sum(qc * kc, 0, keepdims=True)
        o_x = jnp.sum(qc * state, 0, keepdims=True)
        o_ref[:, h * RET_HEAD_DIM:(h + 1) * RET_HEAD_DIM] = inner * v + gam * o_x
        sn_ref[h] = gam * state + kc * v


def _ret_sample(q, k, v, state, gam_tile):
    b = q.shape[0]
    col = lambda a: a.reshape(b, RET_HEADS, RET_HEAD_DIM, 1)
    col_spec = pl.BlockSpec((None, RET_HEADS, RET_HEAD_DIM, 1), lambda i: (i, 0, 0, 0))
    row_spec = pl.BlockSpec((None, 1, RET_WIDTH), lambda i: (i, 0, 0))
    st_spec = pl.BlockSpec((None, RET_HEADS, RET_HEAD_DIM, RET_HEAD_DIM), lambda i: (i, 0, 0, 0))
    o, sn = pl.pallas_call(
        _ret_sample_kernel,
        grid=(b,),
        in_specs=[col_spec, col_spec, row_spec, st_spec,
                  pl.BlockSpec((RET_HEADS, 8, RET_HEAD_DIM), lambda i: (0, 0, 0))],
        out_specs=[row_spec, st_spec],
        out_shape=[jax.ShapeDtypeStruct((b, 1, RET_WIDTH), F32), jax.ShapeDtypeStruct(state.shape, F32)],
        compiler_params=_params("parallel"),
    )(col(q), col(k), v.reshape(b, 1, RET_WIDTH), state, gam_tile)
    return o.reshape(b, RET_WIDTH), sn


def _mixer_out_kernel(attn_ref, ret_ref, gate_ref, x_ref, gn_ref, wout_ref, n2_ref, wpq_ref, keys_ref,
                      h_ref, hn_ref, st_ref):
    gate = gate_ref[...]
    swish = gate * (1.0 / (1.0 + jnp.exp(-gate)))
    normed = []
    for hd in range(RET_HEADS):
        cols = slice(hd * RET_HEAD_DIM, (hd + 1) * RET_HEAD_DIM)
        r = ret_ref[:, cols]
        mu = jnp.mean(r, -1, keepdims=True)
        var = jnp.mean(jnp.square(r - mu), -1, keepdims=True)
        normed.append((r - mu) * lax.rsqrt(var + GN_EPS) * gn_ref[:, cols])
    ret_y = swish * jnp.concatenate(normed, -1)
    h = (x_ref[...]
         + jnp.dot(attn_ref[...].astype(MXU_DTYPE), wout_ref[0:ATTN_WIDTH, :], preferred_element_type=F32)
         + jnp.dot(ret_y.astype(MXU_DTYPE), wout_ref[ATTN_WIDTH:, :], preferred_element_type=F32))
    h_ref[...] = h
    hn = _rms(h, n2_ref[...])
    hn_ref[...] = hn
    hb = hn.astype(MXU_DTYPE)
    nt = (((1,), (1,)), ((), ()))
    for j in range(2 * PEER_HEADS):
        qj = jnp.dot(hb, wpq_ref[:, j * PEER_HALF:(j + 1) * PEER_HALF], preferred_element_type=F32)
        st_ref[j] = lax.dot_general(keys_ref[j], qj.astype(MXU_DTYPE), nt, preferred_element_type=F32)


def _mixer_out(attn, ret_o, gate, x, gn_g, w_out, n2_g, w_pq, keys):
    n = x.shape[0]
    tm = min(256, n)
    half = pl.BlockSpec((tm, ATTN_WIDTH), lambda i: (i, 0))
    full = pl.BlockSpec((tm, D_MODEL), lambda i: (i, 0))
    const = lambda shape: pl.BlockSpec(shape, lambda i: (0,) * len(shape))
    return pl.pallas_call(
        _mixer_out_kernel,
        grid=(n // tm,),
        in_specs=[half] * 3 + [full, const((1, RET_WIDTH)), const((D_MODEL, D_MODEL)), const((1, D_MODEL)),
                               const((D_MODEL, 2 * PEER_HEADS * PEER_HALF)),
                               const((2 * PEER_HEADS, PEER_N_KEYS, PEER_HALF))],
        out_specs=[full, full, pl.BlockSpec((2 * PEER_HEADS, PEER_N_KEYS, tm), lambda i: (0, 0, i))],
        out_shape=[jax.ShapeDtypeStruct((n, D_MODEL), F32), jax.ShapeDtypeStruct((n, D_MODEL), F32),
                   jax.ShapeDtypeStruct((2 * PEER_HEADS, PEER_N_KEYS, n), F32)],
        compiler_params=_params("parallel"),
    )(attn, ret_o, gate, x, gn_g.reshape(1, RET_WIDTH), w_out, n2_g.reshape(1, D_MODEL), w_pq, keys)


def _top_rows(s, k, payload=None):
    nrow = s.shape[0]
    ridx = lax.broadcasted_iota(jnp.int32, s.shape, 0)
    vals, picks = [], []
    for _ in range(k):
        m = jnp.max(s, 0, keepdims=True)
        am = jnp.min(jnp.where(s == m, ridx, nrow), 0, keepdims=True)
        hit = ridx == am
        vals.append(m)
        picks.append(am if payload is None else jnp.sum(jnp.where(hit, payload, 0), 0, keepdims=True))
        s = jnp.where(hit, -jnp.inf, s)
    return jnp.concatenate(vals, 0), jnp.concatenate(picks, 0)


def _product_grid(a0, a1, op):
    pieces = [op(a0[0:1], a1)]
    pieces += [op(a0[a:a + 1], a1[0:8]) for a in range(1, 8)]
    pieces.append(op(a0[8:16], a1[0:1]))
    return jnp.concatenate(pieces, 0)


def _topk_kernel(st_ref, eid_ref, g_ref):
    def head(hd, carry):
        sv0, si0 = _top_rows(st_ref[2 * hd], PEER_TOPK)
        sv1, si1 = _top_rows(st_ref[2 * hd + 1], PEER_TOPK)
        cand = _product_grid(sv0, sv1, jnp.add)
        cidx = _product_grid(si0, si1, lambda i, j: i * PEER_N_KEYS + j)
        best, eid = _top_rows(cand, PEER_TOPK, payload=cidx)
        e = jnp.exp(best - jnp.max(best, 0, keepdims=True))
        rows = pl.ds(pl.multiple_of(hd * PEER_TOPK, PEER_TOPK), PEER_TOPK)
        eid_ref[rows, :] = eid
        g_ref[rows, :] = e / jnp.sum(e, 0, keepdims=True)
        return carry

    lax.fori_loop(0, PEER_HEADS, head, 0)


def _topk(st):
    n = st.shape[-1]
    tn = LANES
    out_spec = pl.BlockSpec((PEER_SLOTS, tn), lambda i: (0, i))
    return pl.pallas_call(
        _topk_kernel,
        grid=(n // tn,),
        in_specs=[pl.BlockSpec((2 * PEER_HEADS, PEER_N_KEYS, tn), lambda i: (0, 0, i))],
        out_specs=[out_spec, out_spec],
        out_shape=[jax.ShapeDtypeStruct((PEER_SLOTS, n), jnp.int32), jax.ShapeDtypeStruct((PEER_SLOTS, n), F32)],
        compiler_params=_params("parallel"),
    )(st)


def _peer_kernel(first_step, eid_hbm, gt_ref, hn_ref, h_ref, nf_ref, uv_hbm, y_ref, eid_smem, buf, sems):
    step = pl.program_id(0) + first_step
    nchunk = PEER_TOKENS_PER_STEP // PEER_CHUNK
    nexp = PEER_CHUNK * PEER_SLOTS
    ids = pltpu.make_async_copy(eid_hbm.at[step], eid_smem, sems.at[2])
    ids.start()
    ids.wait()

    def start_copy(c, i, slot, k):
        e = eid_smem[c * nexp + i]
        src = uv_hbm.at[pl.ds(pl.multiple_of(e * PEER_SLAB, PEER_SLAB), PEER_SLAB), :]
        dst = buf.at[slot, pl.ds(i * PEER_PITCH, PEER_SLAB), :]
        pltpu.make_async_copy(src, dst, sems.at[slot]).start(priority=k % 2)

    def gather(c, slot):
        def group(j, carry):
            for k in range(PEER_ISSUE_UNROLL):
                start_copy(c, j * PEER_ISSUE_UNROLL + k, slot, k)
            return carry
        lax.fori_loop(0, nexp // PEER_ISSUE_UNROLL, group, 0)

    def wait(slot):
        span = pl.ds(0, nexp * PEER_SLAB)
        pltpu.make_async_copy(uv_hbm.at[span, :], buf.at[slot, span, :], sems.at[slot]).wait()

    def rows_of(slot, t, first_row):
        base = t * PEER_SLOTS * PEER_PITCH + first_row
        return jnp.concatenate(
            [jnp.concatenate([buf[slot, pl.ds(base + kg * 8 * PEER_PITCH + s, 8, stride=PEER_PITCH), :]
                              for s in range(D_MODEL // LANES)], axis=1)
             for kg in range(PEER_SLOTS // 8)], axis=0)

    tok_lane = lax.broadcasted_iota(jnp.int32, (PEER_SLOTS, PEER_TOKENS_PER_STEP), 1)
    gather(0, 0)

    def chunk(c, carry):
        slot = c % 2

        @pl.when(c + 1 < nchunk)
        def _():
            gather(c + 1, 1 - slot)

        wait(slot)
        for t in range(PEER_CHUNK):
            tok = c * PEER_CHUNK + t
            x = hn_ref[pl.ds(tok, 1), :]
            a = jax.nn.gelu(jnp.sum(rows_of(slot, t, 0) * x, -1, keepdims=True))
            gate = jnp.sum(jnp.where(tok_lane == tok, gt_ref[...], 0.0), -1, keepdims=True)
            out = jnp.sum((gate * a) * rows_of(slot, t, PEER_SLAB // 2), 0, keepdims=True)
            y_ref[pl.ds(tok, 1), :] = _rms(h_ref[pl.ds(tok, 1), :] + out, nf_ref[...])
        return carry

    lax.fori_loop(0, nchunk, chunk, 0)


def _expert_slabs(peer_u, peer_v):
    e = peer_u.shape[0]
    tiles = lambda a: a.reshape(e, D_MODEL // LANES, LANES)
    return jnp.concatenate([tiles(peer_u), tiles(peer_v)], axis=1).reshape(e * PEER_SLAB, LANES)


def _peer_sc_experts(eid_flat, g_flat, hn, uv_rows, n_sc):
    info = plsc.get_sparse_core_info()
    nw = info.num_cores * info.num_subcores
    nl = info.num_lanes
    assert SC_ROWS == nl and n_sc % nw == 0
    per_w = n_sc // nw
    ngroup = PEER_SLOTS // SC_ROWS
    nvec = D_MODEL // nl
    mesh = plsc.VectorSubcoreMesh(core_axis_name="c", subcore_axis_name="s")
    half = SC_ROWS // 2
    group_buf = pltpu.VMEM((2, half, 2 * D_MODEL), F32)

    @functools.partial(
        pl.kernel, mesh=mesh,
        out_type=jax.ShapeDtypeStruct((n_sc, D_MODEL), F32),
        scratch_types=[pltpu.VMEM((PEER_SLOTS,), jnp.int32), pltpu.VMEM((PEER_SLOTS,), F32),
                       pltpu.VMEM((D_MODEL,), F32), group_buf, group_buf,
                       pltpu.VMEM((D_MODEL,), F32), pltpu.VMEM((SC_ROWS,), F32),
                       pltpu.SemaphoreType.DMA, pltpu.SemaphoreType.DMA],
        compiler_params=pltpu.CompilerParams(needs_layout_passes=False),
    )
    def k(eid_hbm, g_hbm, hn_hbm, uv_hbm, out_hbm, idx_v, g_v, x_v, rows_a, rows_b, o_v, w_v, sem_a, sem_b):
        wid = lax.axis_index("s") * info.num_cores + lax.axis_index("c")
        lane = lax.iota(jnp.int32, nl)

        def group_copies(gi, rows, sem):
            copies = []
            for hf in range(2):
                ids = idx_v.at[pl.ds(pl.multiple_of(gi * SC_ROWS + hf * half, half), half)]
                copies.append(pltpu.make_async_copy(uv_hbm.at[ids], rows.at[hf], sem))
            return copies

        def start_group(gi, rows, sem):
            for cp in group_copies(gi, rows, sem):
                cp.start()

        def wait_group(gi, rows, sem):
            for cp in group_copies(gi, rows, sem):
                cp.wait()

        def reduce_group(gi, rows):
            nacc = 8

            def tree_sum(vs):
                while len(vs) > 1:
                    vs = [vs[i] + vs[i + 1] for i in range(0, len(vs), 2)]
                return vs[0]

            def dot_pair(r, a_vec):
                acc0 = [jnp.zeros((nl,), F32) for _ in range(nacc)]
                acc1 = [jnp.zeros((nl,), F32) for _ in range(nacc)]
                for c in range(nvec):
                    xc = x_v[pl.ds(c * nl, nl)]
                    acc0[c % nacc] = acc0[c % nacc] + rows[0, r, pl.ds(c * nl, nl)] * xc
                    acc1[c % nacc] = acc1[c % nacc] + rows[1, r, pl.ds(c * nl, nl)] * xc
                a_vec = jnp.where(lane == r, jnp.sum(tree_sum(acc0)), a_vec)
                return jnp.where(lane == r + half, jnp.sum(tree_sum(acc1)), a_vec)
            a = lax.fori_loop(0, half, dot_pair, jnp.zeros((nl,), F32))
            z = 0.7978845608028654 * (a + 0.044715 * (a * a * a))
            tanh_z = 1.0 - 2.0 / (1.0 + jnp.exp(2.0 * z))
            gates = g_v[pl.ds(pl.multiple_of(gi * SC_ROWS, SC_ROWS), SC_ROWS)]
            w_v[...] = gates * (0.5 * a * (1.0 + tanh_z))

            ncol = 16
            for cb in range(nvec // ncol):
                def add_rows(r, accs, cb=cb):
                    w0 = plsc.load_gather(w_v, [jnp.full((nl,), r, jnp.int32)])
                    w1 = plsc.load_gather(w_v, [jnp.full((nl,), r + half, jnp.int32)])
                    cols = lambda c: pl.ds(D_MODEL + (cb * ncol + c) * nl, nl)
                    return tuple(acc + (w0 * rows[0, r, cols(c)] + w1 * rows[1, r, cols(c)])
                                 for c, acc in enumerate(accs))
                sums = lax.fori_loop(0, half, add_rows, tuple(jnp.zeros((nl,), F32) for _ in range(ncol)))
                for c, v in enumerate(sums):
                    plsc.addupdate(o_v.at[pl.ds((cb * ncol + c) * nl, nl)], v)

        @pl.loop(0, per_w)
        def _(ti):
            tok = wid * per_w + ti
            base = pl.multiple_of(tok * PEER_SLOTS, PEER_SLOTS)
            pltpu.sync_copy(eid_hbm.at[pl.ds(base, PEER_SLOTS)], idx_v)
            start_group(0, rows_a, sem_a)
            pltpu.sync_copy(g_hbm.at[pl.ds(base, PEER_SLOTS)], g_v)
            pltpu.sync_copy(hn_hbm.at[tok], x_v)
            for c in range(nvec):
                o_v[pl.ds(c * nl, nl)] = jnp.zeros((nl,), F32)

            @pl.loop(0, ngroup // 2)
            def _(j):
                start_group(2 * j + 1, rows_b, sem_b)
                wait_group(2 * j, rows_a, sem_a)
                reduce_group(2 * j, rows_a)

                @pl.when(j + 1 < ngroup // 2)
                def _():
                    start_group(2 * j + 2, rows_a, sem_a)

                wait_group(2 * j + 1, rows_b, sem_b)
                reduce_group(2 * j + 1, rows_b)

            pltpu.sync_copy(o_v, out_hbm.at[tok])

    return k(eid_flat, g_flat, hn, uv_rows)


def _residual_norm_kernel(h_ref, p_ref, nf_ref, y_ref):
    y_ref[...] = _rms(h_ref[...] + p_ref[...], nf_ref[...])


def _peer(eid_t, g_t, hn, h, nf_g, uv):
    n = hn.shape[0]
    tp = PEER_TOKENS_PER_STEP
    n_sc = SC_TOKENS if n > SC_TOKENS else 0
    eid_rows = eid_t.T
    eid_tok = eid_rows.reshape(n // tp, tp * PEER_SLOTS)
    rows = PEER_CHUNK * PEER_SLOTS * PEER_PITCH
    first = n_sc // tp
    tok = pl.BlockSpec((tp, D_MODEL), lambda i: (i + first, 0))
    y_tc = pl.pallas_call(
        functools.partial(_peer_kernel, first),
        grid=((n - n_sc) // tp,),
        in_specs=[pl.BlockSpec(memory_space=pl.ANY),
                  pl.BlockSpec((PEER_SLOTS, tp), lambda i: (0, i + first)),
                  tok, tok,
                  pl.BlockSpec((1, D_MODEL), lambda i: (0, 0)),
                  pl.BlockSpec(memory_space=pl.ANY)],
        out_specs=pl.BlockSpec((tp, D_MODEL), lambda i: (i, 0)),
        out_shape=jax.ShapeDtypeStruct((n - n_sc, D_MODEL), F32),
        scratch_shapes=[pltpu.SMEM((tp * PEER_SLOTS,), jnp.int32),
                        pltpu.VMEM((2, rows, LANES), F32),
                        pltpu.SemaphoreType.DMA((3,))],
        compiler_params=_params("arbitrary"),
    )(eid_tok, g_t, hn, h, nf_g.reshape(1, D_MODEL), uv)
    if n_sc == 0:
        return y_tc
    p_sc = _peer_sc_experts(eid_rows.reshape(-1), g_t.T.reshape(-1), hn,
                            uv.reshape(-1, 2 * D_MODEL), n_sc)
    tm = 256
    blk = pl.BlockSpec((tm, D_MODEL), lambda i: (i, 0))
    y_sc = pl.pallas_call(
        _residual_norm_kernel,
        grid=(n_sc // tm,),
        in_specs=[blk, blk, pl.BlockSpec((1, D_MODEL), lambda i: (0, 0))],
        out_specs=blk,
        out_shape=jax.ShapeDtypeStruct((n_sc, D_MODEL), F32),
        compiler_params=_params("parallel"),
    )(h, p_sc, nf_g.reshape(1, D_MODEL))
    return jnp.concatenate([y_sc, y_tc], axis=0)


def _alibi_slopes():
    return (2.0 ** (-8.0 * (np.arange(ATTN_HEADS) + 1) / ATTN_HEADS)).astype(np.float32)


def _ret_consts():
    log_gamma = np.log(1.0 - 2.0 ** (-5.0 - np.arange(RET_HEADS))).astype(np.float32)
    pos = np.arange(RET_CHUNK, dtype=np.float32)
    rel = pos[:, None] - pos[None, :]
    dmat = np.where(rel[None] >= 0, np.exp(np.maximum(rel, 0.0)[None] * log_gamma[:, None, None]), 0.0)
    lanes = lambda col: np.broadcast_to(col[:, :, None], (RET_HEADS, col.shape[1], RET_HEAD_DIM))
    qdec = lanes(np.exp((pos[None, :] + 1.0) * log_gamma[:, None]))
    kdec = lanes(np.exp((RET_CHUNK - 1.0 - pos)[None, :] * log_gamma[:, None]))
    cdec = lanes(np.broadcast_to(np.exp(RET_CHUNK * log_gamma)[:, None], (RET_HEADS, 8)))
    gam = lanes(np.broadcast_to(np.exp(log_gamma)[:, None], (RET_HEADS, 8)))
    as_f32 = lambda a: jnp.asarray(np.ascontiguousarray(a, dtype=np.float32))
    return tuple(as_f32(a) for a in (dmat, qdec, kdec, cdec)), as_f32(gam)


def _channel_and_norm(attn, ret_o, gate, x, gn_g, w_out, n2_g, w_pq, keys, uv, nf_g):
    h, hn, st = _mixer_out(attn, ret_o, gate, x, gn_g, w_out, n2_g, w_pq, keys)
    eid_t, g_t = _topk(st)
    return _peer(eid_t, g_t, hn, h, nf_g, uv)


def kernel(x_prompt, x_sample, cache_k_win, cache_v_win, state_ret, norm1_g, w_in, ret_gn_g, w_out, norm2_g, w_pq, peer_sub_keys, peer_u, peer_v, norm_f_g):
    depth = w_in.shape[0]
    assert depth == 1, "single-layer stack"
    b, s, _ = x_prompt.shape
    bs = x_sample.shape[0]
    assert x_sample.shape[1] == 1 and s % (ATTN_BLOCK * max(DILATIONS)) == 0
    assert cache_k_win.shape[2] % (N_STEPS * max(DILATIONS)) == 0

    slopes = _alibi_slopes()
    slopes_tile = jnp.asarray(np.broadcast_to(slopes.reshape(ATTN_HEADS // 2, 2, 1, 1), (ATTN_HEADS // 2, 2, 8, LANES)).copy())
    ret_consts, gam_tile = _ret_consts()

    w_in_b = w_in[0].astype(MXU_DTYPE)
    w_out_b = w_out[0].astype(MXU_DTYPE)
    w_pq_b = w_pq[0].astype(MXU_DTYPE)
    keys_b = peer_sub_keys[0].reshape(2 * PEER_HEADS, PEER_N_KEYS, PEER_HALF).astype(MXU_DTYPE)
    uv = _expert_slabs(peer_u[0], peer_v[0])
    tail = (ret_gn_g[0], w_out_b, norm2_g[0], w_pq_b, keys_b, uv, norm_f_g)

    xp = x_prompt.reshape(b * s, D_MODEL)
    aq, ak, av, rq, rk, rv, rg, kt, vt = _inproj(xp, norm1_g[0], w_in_b, seq_len=s)
    seq = lambda a: a.reshape(b, s, ATTN_WIDTH)
    attn = _prompt_attention(seq(aq), seq(ak), seq(av), slopes_tile)
    ret_o, s_fin = _ret_prompt(seq(rq), seq(rk), seq(rv), ret_consts)
    y_prompt = _channel_and_norm(attn.reshape(b * s, ATTN_WIDTH), ret_o.reshape(b * s, RET_WIDTH), rg, xp, *tail)
    keep = min(N_STEPS * max(DILATIONS), s)
    window = lambda a: a.reshape(b, ATTN_HEADS, ATTN_HEAD_DIM, s).transpose(0, 3, 1, 2)[:, s - keep:]
    k_win, v_win = window(kt), window(vt)

    xs = x_sample.reshape(bs, D_MODEL)
    aq, ak, av, rq, rk, rv, rg = _inproj(xs, norm1_g[0], w_in_b)
    position_minor = lambda c: c.transpose(0, 2, 3, 1)
    attn = _sample_attention(aq, ak, av, position_minor(cache_k_win[0]), position_minor(cache_v_win[0]))
    ret_o, s_new = _ret_sample(rq, rk, rv, state_ret[0], gam_tile)
    y_sample = _channel_and_norm(attn, ret_o, rg, xs, *tail)

    return (y_prompt.reshape(b, s, D_MODEL), y_sample.reshape(bs, 1, D_MODEL),
            k_win[None], v_win[None], s_fin[None],
            ak.reshape(1, bs, 1, ATTN_HEADS, ATTN_HEAD_DIM), av.reshape(1, bs, 1, ATTN_HEADS, ATTN_HEAD_DIM),
            s_new[None])
```

```python
import functools

import numpy as np
import jax
import jax.numpy as jnp
from jax import lax
from jax.experimental import pallas as pl
from jax.experimental.pallas import tpu as pltpu
from jax.experimental.pallas import tpu_sc as plsc

F32 = jnp.float32
MXU_DTYPE = jnp.bfloat16

D_MODEL = 1024
ATTN_WIDTH = 512
RET_WIDTH = 512
ATTN_HEAD_DIM = 64
ATTN_HEADS = 8
RET_HEADS = 4
RET_HEAD_DIM = 128
DILATIONS = (1, 4, 16)
N_STEPS = 128
ATTN_BLOCK = 128
ATTN_PAD = N_STEPS * max(DILATIONS)
RET_CHUNK = 128
PEER_HEADS = 8
PEER_N_KEYS = 128
PEER_TOPK = 16
PEER_HALF = 128
PEER_SLOTS = PEER_HEADS * PEER_TOPK
NORM_EPS = 1e-6
GN_EPS = 1e-5
IN_SPLITS = 7
RET_K_SCALE = RET_HEAD_DIM ** -0.5
ATTN_SCALE = ATTN_HEAD_DIM ** -0.5
LANES = 128
VMEM_LIMIT = 48 * 1024 * 1024

PEER_TOKENS_PER_STEP = 128
PEER_CHUNK = 8
PEER_SLAB = 2 * D_MODEL // LANES
PEER_PITCH = PEER_SLAB + 1
PEER_ISSUE_UNROLL = 8
SC_TOKENS = 14592
SC_ROWS = 16


def _params(*sem):
    return pltpu.CompilerParams(dimension_semantics=sem, vmem_limit_bytes=VMEM_LIMIT)


def _rms(x, g):
    r = lax.rsqrt(jnp.mean(x * x, axis=-1, keepdims=True) + NORM_EPS)
    return x * r * g


def _inproj_kernel(x_ref, g_ref, w_ref, *refs):
    out_refs = refs[-IN_SPLITS:] if len(refs) == IN_SPLITS else refs[1:1 + IN_SPLITS]
    xn = _rms(x_ref[...], g_ref[...]).astype(MXU_DTYPE)
    for i, o_ref in enumerate(out_refs):
        y = jnp.dot(xn, w_ref[:, i * ATTN_WIDTH:(i + 1) * ATTN_WIDTH], preferred_element_type=F32)
        if i == 4:
            y = y * RET_K_SCALE
        o_ref[...] = y
    if len(refs) > IN_SPLITS:
        wkv_t_ref, kt_ref, vt_ref = refs[0], refs[-2], refs[-1]
        nt = (((1,), (1,)), ((), ()))
        kv_t = lax.dot_general(wkv_t_ref[...], xn, nt, preferred_element_type=F32)
        kt_ref[...] = kv_t[:ATTN_WIDTH]
        vt_ref[...] = kv_t[ATTN_WIDTH:]


def _inproj(x, g, w, seq_len=None):
    n = x.shape[0]
    tm = min(256, n)
    out = jax.ShapeDtypeStruct((n, ATTN_WIDTH), F32)
    in_specs = [pl.BlockSpec((tm, D_MODEL), lambda i: (i, 0)),
                pl.BlockSpec((1, D_MODEL), lambda i: (0, 0)),
                pl.BlockSpec((D_MODEL, IN_SPLITS * ATTN_WIDTH), lambda i: (0, 0))]
    out_specs = [pl.BlockSpec((tm, ATTN_WIDTH), lambda i: (i, 0))] * IN_SPLITS
    out_shape = [out] * IN_SPLITS
    args = [x, g.reshape(1, D_MODEL), w]
    if seq_len is not None:
        per_row = seq_len // tm
        in_specs.append(pl.BlockSpec((2 * ATTN_WIDTH, D_MODEL), lambda i: (0, 0)))
        args.append(w[:, ATTN_WIDTH:3 * ATTN_WIDTH].T)
        out_specs += [pl.BlockSpec((None, ATTN_WIDTH, tm), lambda i: (i // per_row, 0, i % per_row))] * 2
        out_shape += [jax.ShapeDtypeStruct((n // seq_len, ATTN_WIDTH, seq_len), F32)] * 2
    return pl.pallas_call(
        _inproj_kernel,
        grid=(n // tm,),
        in_specs=in_specs,
        out_specs=out_specs,
        out_shape=out_shape,
        compiler_params=_params("parallel"),
    )(*args)


def _merge_groups(outs, lses):
    m = functools.reduce(jnp.maximum, lses)
    es = [jnp.exp(l - m) for l in lses]
    den = functools.reduce(jnp.add, es)
    return functools.reduce(jnp.add, [(e / den) * o for e, o in zip(es, outs)])


def _prompt_attn_kernel(s_len, q_ref, k_ref, v_ref, slope_ref, o_ref, og_ref, lg_ref):
    lane = lax.broadcasted_iota(jnp.int32, (ATTN_BLOCK, LANES), 1)
    first = lane < ATTN_HEAD_DIM
    qi = lax.broadcasted_iota(jnp.int32, (ATTN_BLOCK, 2 * ATTN_BLOCK), 0)
    kj = lax.broadcasted_iota(jnp.int32, (ATTN_BLOCK, 2 * ATTN_BLOCK), 1)
    back = qi + ATTN_BLOCK - kj
    in_band = (back >= 0) & (back <= N_STEPS)
    nt = (((1,), (1,)), ((), ()))
    ntile = s_len // ATTN_BLOCK
    for g, dil in enumerate(DILATIONS):
        dist = (back * dil).astype(F32)
        bias = [jnp.where(in_band, -slope_ref[hh, 0:1, 0:1] * dist, -jnp.inf) for hh in range(2)]

        def tile(t, carry, g=g, dil=dil, bias=bias):
            r, n = t % dil, t // dil
            qrow = r + n * (dil * ATTN_BLOCK)
            krow = ATTN_PAD + qrow - dil * ATTN_BLOCK
            q = q_ref[pl.ds(qrow, ATTN_BLOCK, stride=dil), :]
            kb = k_ref[pl.ds(krow, 2 * ATTN_BLOCK, stride=dil), :].astype(MXU_DTYPE)
            vb = v_ref[pl.ds(krow, 2 * ATTN_BLOCK, stride=dil), :].astype(MXU_DTYPE)
            exists = kj + n * ATTN_BLOCK >= ATTN_BLOCK
            outs, lses = [], []
            for hh in range(2):
                qm = jnp.where(first if hh == 0 else ~first, q, 0.0).astype(MXU_DTYPE)
                sc = lax.dot_general(qm, kb, nt, preferred_element_type=F32) * ATTN_SCALE + bias[hh]
                sc = jnp.where(exists, sc, -jnp.inf)
                m = jnp.max(sc, -1, keepdims=True)
                p = jnp.exp(sc - m)
                den = jnp.sum(p, -1, keepdims=True)
                outs.append(jnp.dot((p / den).astype(MXU_DTYPE), vb, preferred_element_type=F32))
                lses.append(jnp.broadcast_to(m + jnp.log(den), (ATTN_BLOCK, LANES)))
            og_ref[g, pl.ds(qrow, ATTN_BLOCK, stride=dil), :] = jnp.where(first, outs[0], outs[1])
            lg_ref[g, pl.ds(qrow, ATTN_BLOCK, stride=dil), :] = jnp.where(first, lses[0], lses[1])
            return carry

        lax.fori_loop(0, ntile, tile, 0, unroll=16)

    def merge(i, carry):
        rows = pl.ds(pl.multiple_of(i * ATTN_BLOCK, ATTN_BLOCK), ATTN_BLOCK)
        groups = range(len(DILATIONS))
        o_ref[rows, :] = _merge_groups([og_ref[g, rows, :] for g in groups], [lg_ref[g, rows, :] for g in groups])
        return carry

    lax.fori_loop(0, ntile, merge, 0, unroll=4)


def _prompt_attention(q, k, v, slopes_tile):
    b, s, _ = q.shape
    pad = lambda a: jnp.pad(a, ((0, 0), (ATTN_PAD, 0), (0, 0)))
    q_spec = pl.BlockSpec((None, s, LANES), lambda i, h: (i, 0, h))
    kv_spec = pl.BlockSpec((None, ATTN_PAD + s, LANES), lambda i, h: (i, 0, h))
    return pl.pallas_call(
        functools.partial(_prompt_attn_kernel, s),
        grid=(b, ATTN_WIDTH // LANES),
        in_specs=[q_spec, kv_spec, kv_spec, pl.BlockSpec((None, 2, 8, LANES), lambda i, h: (h, 0, 0, 0))],
        out_specs=q_spec,
        out_shape=jax.ShapeDtypeStruct((b, s, ATTN_WIDTH), F32),
        scratch_shapes=[pltpu.VMEM((len(DILATIONS), s, LANES), F32), pltpu.VMEM((len(DILATIONS), s, LANES), F32)],
        compiler_params=_params("parallel", "parallel"),
    )(q, pad(k), pad(v), slopes_tile)


def _sample_attn_kernel(slopes, q_ref, kn_ref, vn_ref, kt_ref, vt_ref, o_ref):
    w = kt_ref.shape[-1]
    back_i = w - lax.broadcasted_iota(jnp.int32, (1, w), 1)
    back = back_i.astype(F32)
    valid = [(back_i % dil == 0) & (back_i <= N_STEPS * dil) for dil in DILATIONS]
    for hd in range(ATTN_HEADS):
        q = q_ref[hd]
        raw = jnp.sum(kt_ref[hd] * q, 0, keepdims=True) * ATTN_SCALE - slopes[hd] * back
        l0 = jnp.sum(q * kn_ref[hd], 0, keepdims=True) * ATTN_SCALE
        vt = vt_ref[hd]
        outs, lses = [], []
        for ok in valid:
            m = jnp.maximum(jnp.max(jnp.where(ok, raw, -jnp.inf), -1, keepdims=True), l0)
            p = jnp.where(ok, jnp.exp(raw - m), 0.0)
            p0 = jnp.exp(l0 - m)
            s = jnp.sum(p, -1, keepdims=True) + p0
            outs.append((jnp.sum(vt * p, -1, keepdims=True) + p0 * vn_ref[hd]) / s)
            lses.append(m + jnp.log(s))
        o_ref[hd] = _merge_groups(outs, lses)


def _sample_attention(q, kn, vn, cache_kt, cache_vt):
    b = q.shape[0]
    assert cache_kt.shape[-1] >= N_STEPS * max(DILATIONS)
    cols = lambda a: a.reshape(b, ATTN_HEADS, ATTN_HEAD_DIM, 1)
    col_spec = pl.BlockSpec((None, ATTN_HEADS, ATTN_HEAD_DIM, 1), lambda i: (i, 0, 0, 0))
    win_spec = pl.BlockSpec((None,) + cache_kt.shape[1:], lambda i: (i, 0, 0, 0))
    out = pl.pallas_call(
        functools.partial(_sample_attn_kernel, [float(v) for v in _alibi_slopes()]),
        grid=(b,),
        in_specs=[col_spec, col_spec, col_spec, win_spec, win_spec],
        out_specs=col_spec,
        out_shape=jax.ShapeDtypeStruct((b, ATTN_HEADS, ATTN_HEAD_DIM, 1), F32),
        compiler_params=_params("parallel"),
    )(cols(q), cols(kn), cols(vn), cache_kt, cache_vt)
    return out.reshape(b, ATTN_WIDTH)


def _ret_prompt_kernel(nchunk, q_ref, k_ref, v_ref, dmat_ref, qdec_ref, kdec_ref, cdec_ref, o_ref, s_ref):
    s_ref[...] = jnp.zeros_like(s_ref)
    nt = (((1,), (1,)), ((), ()))
    tn = (((0,), (0,)), ((), ()))

    def chunk(c, carry):
        row = pl.multiple_of(c * RET_CHUNK, RET_CHUNK)
        q = q_ref[pl.ds(row, RET_CHUNK), :]
        k = k_ref[pl.ds(row, RET_CHUNK), :]
        v = v_ref[pl.ds(row, RET_CHUNK), :].astype(MXU_DTYPE)
        qb = q.astype(MXU_DTYPE)
        state = s_ref[...]
        inner = lax.dot_general(qb, k.astype(MXU_DTYPE), nt, preferred_element_type=F32) * dmat_ref[...]
        o_in = jnp.dot(inner.astype(MXU_DTYPE), v, preferred_element_type=F32)
        o_x = jnp.dot(qb, state.astype(MXU_DTYPE), preferred_element_type=F32) * qdec_ref[...]
        o_ref[pl.ds(row, RET_CHUNK), :] = o_in + o_x
        kd = (k * kdec_ref[...]).astype(MXU_DTYPE)
        s_ref[...] = cdec_ref[0:1, :] * state + lax.dot_general(kd, v, tn, preferred_element_type=F32)
        return carry

    lax.fori_loop(0, nchunk, chunk, 0, unroll=8)


def _ret_prompt(q, k, v, consts):
    b, s, _ = q.shape
    seq = pl.BlockSpec((None, s, RET_HEAD_DIM), lambda i, h: (i, 0, h))
    per_head = lambda rows: pl.BlockSpec((None, rows, RET_HEAD_DIM), lambda i, h: (h, 0, 0))
    return pl.pallas_call(
        functools.partial(_ret_prompt_kernel, s // RET_CHUNK),
        grid=(b, RET_HEADS),
        in_specs=[seq, seq, seq, per_head(RET_CHUNK), per_head(RET_CHUNK), per_head(RET_CHUNK), per_head(8)],
        out_specs=[seq, pl.BlockSpec((None, None, RET_HEAD_DIM, RET_HEAD_DIM), lambda i, h: (i, h, 0, 0))],
        out_shape=[jax.ShapeDtypeStruct((b, s, RET_WIDTH), F32),
                   jax.ShapeDtypeStruct((b, RET_HEADS, RET_HEAD_DIM, RET_HEAD_DIM), F32)],
        compiler_params=_params("parallel", "parallel"),
    )(q, k, v, *consts)


def _ret_sample_kernel(qc_ref, kc_ref, v_ref, s_ref, gam_ref, o_ref, sn_ref):
    for h in range(RET_HEADS):
        qc = qc_ref[h]
        kc = kc_ref[h]
        v = v_ref[:, h * RET_HEAD_DIM:(h + 1) * RET_HEAD_DIM]
        gam = gam_ref[h, 0:1, :]
        state = s_ref[h]
        inner = jnp.sum(qc * kc, 0, keepdims=True)
        o_x = jnp.sum(qc * state, 0, keepdims=True)
        o_ref[:, h * RET_HEAD_DIM:(h + 1) * RET_HEAD_DIM] = inner * v + gam * o_x
        sn_ref[h] = gam * state + kc * v


def _ret_sample(q, k, v, state, gam_tile):
    b = q.shape[0]
    col = lambda a: a.reshape(b, RET_HEADS, RET_HEAD_DIM, 1)
    col_spec = pl.BlockSpec((None, RET_HEADS, RET_HEAD_DIM, 1), lambda i: (i, 0, 0, 0))
    row_spec = pl.BlockSpec((None, 1, RET_WIDTH), lambda i: (i, 0, 0))
    st_spec = pl.BlockSpec((None, RET_HEADS, RET_HEAD_DIM, RET_HEAD_DIM), lambda i: (i, 0, 0, 0))
    o, sn = pl.pallas_call(
        _ret_sample_kernel,
        grid=(b,),
        in_specs=[col_spec, col_spec, row_spec, st_spec,
                  pl.BlockSpec((RET_HEADS, 8, RET_HEAD_DIM), lambda i: (0, 0, 0))],
        out_specs=[row_spec, st_spec],
        out_shape=[jax.ShapeDtypeStruct((b, 1, RET_WIDTH), F32), jax.ShapeDtypeStruct(state.shape, F32)],
        compiler_params=_params("parallel"),
    )(col(q), col(k), v.reshape(b, 1, RET_WIDTH), state, gam_tile)
    return o.reshape(b, RET_WIDTH), sn


def _mixer_out_kernel(attn_ref, ret_ref, gate_ref, x_ref, gn_ref, wout_ref, n2_ref, wpq_ref, keys_ref,
                      h_ref, hn_ref, st_ref):
    gate = gate_ref[...]
    swish = gate * (1.0 / (1.0 + jnp.exp(-gate)))
    normed = []
    for hd in range(RET_HEADS):
        cols = slice(hd * RET_HEAD_DIM, (hd + 1) * RET_HEAD_DIM)
        r = ret_ref[:, cols]
        mu = jnp.mean(r, -1, keepdims=True)
        var = jnp.mean(jnp.square(r - mu), -1, keepdims=True)
        normed.append((r - mu) * lax.rsqrt(var + GN_EPS) * gn_ref[:, cols])
    ret_y = swish * jnp.concatenate(normed, -1)
    h = (x_ref[...]
         + jnp.dot(attn_ref[...].astype(MXU_DTYPE), wout_ref[0:ATTN_WIDTH, :], preferred_element_type=F32)
         + jnp.dot(ret_y.astype(MXU_DTYPE), wout_ref[ATTN_WIDTH:, :], preferred_element_type=F32))
    h_ref[...] = h
    hn = _rms(h, n2_ref[...])
    hn_ref[...] = hn
    hb = hn.astype(MXU_DTYPE)
    nt = (((1,), (1,)), ((), ()))
    for j in range(2 * PEER_HEADS):
        qj = jnp.dot(hb, wpq_ref[:, j * PEER_HALF:(j + 1) * PEER_HALF], preferred_element_type=F32)
        st_ref[j] = lax.dot_general(keys_ref[j], qj.astype(MXU_DTYPE), nt, preferred_element_type=F32)


def _mixer_out(attn, ret_o, gate, x, gn_g, w_out, n2_g, w_pq, keys):
    n = x.shape[0]
    tm = min(256, n)
    half = pl.BlockSpec((tm, ATTN_WIDTH), lambda i: (i, 0))
    full = pl.BlockSpec((tm, D_MODEL), lambda i: (i, 0))
    const = lambda shape: pl.BlockSpec(shape, lambda i: (0,) * len(shape))
    return pl.pallas_call(
        _mixer_out_kernel,
        grid=(n // tm,),
        in_specs=[half] * 3 + [full, const((1, RET_WIDTH)), const((D_MODEL, D_MODEL)), const((1, D_MODEL)),
                               const((D_MODEL, 2 * PEER_HEADS * PEER_HALF)),
                               const((2 * PEER_HEADS, PEER_N_KEYS, PEER_HALF))],
        out_specs=[full, full, pl.BlockSpec((2 * PEER_HEADS, PEER_N_KEYS, tm), lambda i: (0, 0, i))],
        out_shape=[jax.ShapeDtypeStruct((n, D_MODEL), F32), jax.ShapeDtypeStruct((n, D_MODEL), F32),
                   jax.ShapeDtypeStruct((2 * PEER_HEADS, PEER_N_KEYS, n), F32)],
        compiler_params=_params("parallel"),
    )(attn, ret_o, gate, x, gn_g.reshape(1, RET_WIDTH), w_out, n2_g.reshape(1, D_MODEL), w_pq, keys)


def _top_rows(s, k, payload=None):
    nrow = s.shape[0]
    ridx = lax.broadcasted_iota(jnp.int32, s.shape, 0)
    vals, picks = [], []
    for _ in range(k):
        m = jnp.max(s, 0, keepdims=True)
        am = jnp.min(jnp.where(s == m, ridx, nrow), 0, keepdims=True)
        hit = ridx == am
        vals.append(m)
        picks.append(am if payload is None else jnp.sum(jnp.where(hit, payload, 0), 0, keepdims=True))
        s = jnp.where(hit, -jnp.inf, s)
    return jnp.concatenate(vals, 0), jnp.concatenate(picks, 0)


def _product_grid(a0, a1, op):
    pieces = [op(a0[0:1], a1)]
    pieces += [op(a0[a:a + 1], a1[0:8]) for a in range(1, 8)]
    pieces.append(op(a0[8:16], a1[0:1]))
    return jnp.concatenate(pieces, 0)


def _topk_kernel(st_ref, eid_ref, g_ref):
    def head(hd, carry):
        sv0, si0 = _top_rows(st_ref[2 * hd], PEER_TOPK)
        sv1, si1 = _top_rows(st_ref[2 * hd + 1], PEER_TOPK)
        cand = _product_grid(sv0, sv1, jnp.add)
        cidx = _product_grid(si0, si1, lambda i, j: i * PEER_N_KEYS + j)
        best, eid = _top_rows(cand, PEER_TOPK, payload=cidx)
        e = jnp.exp(best - jnp.max(best, 0, keepdims=True))
        rows = pl.ds(pl.multiple_of(hd * PEER_TOPK, PEER_TOPK), PEER_TOPK)
        eid_ref[rows, :] = eid
        g_ref[rows, :] = e / jnp.sum(e, 0, keepdims=True)
        return carry

    lax.fori_loop(0, PEER_HEADS, head, 0, unroll=8)


def _topk(st):
    n = st.shape[-1]
    tn = LANES
    out_spec = pl.BlockSpec((PEER_SLOTS, tn), lambda i: (0, i))
    return pl.pallas_call(
        _topk_kernel,
        grid=(n // tn,),
        in_specs=[pl.BlockSpec((2 * PEER_HEADS, PEER_N_KEYS, tn), lambda i: (0, 0, i))],
        out_specs=[out_spec, out_spec],
        out_shape=[jax.ShapeDtypeStruct((PEER_SLOTS, n), jnp.int32), jax.ShapeDtypeStruct((PEER_SLOTS, n), F32)],
        compiler_params=_params("parallel"),
    )(st)


def _peer_kernel(first_step, eid_hbm, gt_ref, hn_ref, h_ref, nf_ref, uv_hbm, y_ref, eid_smem, buf, sems):
    step = pl.program_id(0) + first_step
    nchunk = PEER_TOKENS_PER_STEP // PEER_CHUNK
    nexp = PEER_CHUNK * PEER_SLOTS
    ids = pltpu.make_async_copy(eid_hbm.at[step], eid_smem, sems.at[2])
    ids.start()
    ids.wait()

    def start_copy(c, i, slot, k):
        e = eid_smem[c * nexp + i]
        src = uv_hbm.at[pl.ds(pl.multiple_of(e * PEER_SLAB, PEER_SLAB), PEER_SLAB), :]
        dst = buf.at[slot, pl.ds(i * PEER_PITCH, PEER_SLAB), :]
        pltpu.make_async_copy(src, dst, sems.at[slot]).start(priority=k % 2)

    def gather(c, slot):
        def group(j, carry):
            for k in range(PEER_ISSUE_UNROLL):
                start_copy(c, j * PEER_ISSUE_UNROLL + k, slot, k)
            return carry
        lax.fori_loop(0, nexp // PEER_ISSUE_UNROLL, group, 0)

    def wait(slot):
        span = pl.ds(0, nexp * PEER_SLAB)
        pltpu.make_async_copy(uv_hbm.at[span, :], buf.at[slot, span, :], sems.at[slot]).wait()

    def rows_of(slot, t, first_row):
        base = t * PEER_SLOTS * PEER_PITCH + first_row
        return jnp.concatenate(
            [jnp.concatenate([buf[slot, pl.ds(base + kg * 8 * PEER_PITCH + s, 8, stride=PEER_PITCH), :]
                              for s in range(D_MODEL // LANES)], axis=1)
             for kg in range(PEER_SLOTS // 8)], axis=0)

    tok_lane = lax.broadcasted_iota(jnp.int32, (PEER_SLOTS, PEER_TOKENS_PER_STEP), 1)
    gather(0, 0)

    def chunk(c, carry):
        slot = c % 2

        @pl.when(c + 1 < nchunk)
        def _():
            gather(c + 1, 1 - slot)

        wait(slot)
        for t in range(PEER_CHUNK):
            tok = c * PEER_CHUNK + t
            x = hn_ref[pl.ds(tok, 1), :]
            a = jax.nn.gelu(jnp.sum(rows_of(slot, t, 0) * x, -1, keepdims=True))
            gate = jnp.sum(jnp.where(tok_lane == tok, gt_ref[...], 0.0), -1, keepdims=True)
            out = jnp.sum((gate * a) * rows_of(slot, t, PEER_SLAB // 2), 0, keepdims=True)
            y_ref[pl.ds(tok, 1), :] = _rms(h_ref[pl.ds(tok, 1), :] + out, nf_ref[...])
        return carry

    lax.fori_loop(0, nchunk, chunk, 0)


def _expert_slabs(peer_u, peer_v):
    e = peer_u.shape[0]
    tiles = lambda a: a.reshape(e, D_MODEL // LANES, LANES)
    return jnp.concatenate([tiles(peer_u), tiles(peer_v)], axis=1).reshape(e * PEER_SLAB, LANES)


def _peer_sc_experts(eid_flat, g_flat, hn, uv_rows, n_sc):
    info = plsc.get_sparse_core_info()
    nw = info.num_cores * info.num_subcores
    nl = info.num_lanes
    assert SC_ROWS == nl and n_sc % nw == 0
    per_w = n_sc // nw
    ngroup = PEER_SLOTS // SC_ROWS
    nvec = D_MODEL // nl
    mesh = plsc.VectorSubcoreMesh(core_axis_name="c", subcore_axis_name="s")
    half = SC_ROWS // 2
    group_buf = pltpu.VMEM((2, half, 2 * D_MODEL), F32)

    @functools.partial(
        pl.kernel, mesh=mesh,
        out_type=jax.ShapeDtypeStruct((n_sc, D_MODEL), F32),
        scratch_types=[pltpu.VMEM((PEER_SLOTS,), jnp.int32), pltpu.VMEM((PEER_SLOTS,), F32),
                       pltpu.VMEM((D_MODEL,), F32), group_buf, group_buf,
                       pltpu.VMEM((D_MODEL,), F32), pltpu.VMEM((SC_ROWS,), F32),
                       pltpu.SemaphoreType.DMA, pltpu.SemaphoreType.DMA],
        compiler_params=pltpu.CompilerParams(needs_layout_passes=False),
    )
    def k(eid_hbm, g_hbm, hn_hbm, uv_hbm, out_hbm, idx_v, g_v, x_v, rows_a, rows_b, o_v, w_v, sem_a, sem_b):
        wid = lax.axis_index("s") * info.num_cores + lax.axis_index("c")
        lane = lax.iota(jnp.int32, nl)

        def group_copies(gi, rows, sem):
            copies = []
            for hf in range(2):
                ids = idx_v.at[pl.ds(pl.multiple_of(gi * SC_ROWS + hf * half, half), half)]
                copies.append(pltpu.make_async_copy(uv_hbm.at[ids], rows.at[hf], sem))
            return copies

        def start_group(gi, rows, sem):
            for cp in group_copies(gi, rows, sem):
                cp.start()

        def wait_group(gi, rows, sem):
            for cp in group_copies(gi, rows, sem):
                cp.wait()

        def reduce_group(gi, rows):
            nacc = 8

            def tree_sum(vs):
                while len(vs) > 1:
                    vs = [vs[i] + vs[i + 1] for i in range(0, len(vs), 2)]
                return vs[0]

            def dot_pair(r, a_vec):
                acc0 = [jnp.zeros((nl,), F32) for _ in range(nacc)]
                acc1 = [jnp.zeros((nl,), F32) for _ in range(nacc)]
                for c in range(nvec):
                    xc = x_v[pl.ds(c * nl, nl)]
                    acc0[c % nacc] = acc0[c % nacc] + rows[0, r, pl.ds(c * nl, nl)] * xc
                    acc1[c % nacc] = acc1[c % nacc] + rows[1, r, pl.ds(c * nl, nl)] * xc
                a_vec = jnp.where(lane == r, jnp.sum(tree_sum(acc0)), a_vec)
                return jnp.where(lane == r + half, jnp.sum(tree_sum(acc1)), a_vec)
            a = lax.fori_loop(0, half, dot_pair, jnp.zeros((nl,), F32))
            z = 0.7978845608028654 * (a + 0.044715 * (a * a * a))
            tanh_z = 1.0 - 2.0 / (1.0 + jnp.exp(2.0 * z))
            gates = g_v[pl.ds(pl.multiple_of(gi * SC_ROWS, SC_ROWS), SC_ROWS)]
            w_v[...] = gates * (0.5 * a * (1.0 + tanh_z))

            ncol = 16
            for cb in range(nvec // ncol):
                def add_rows(r, accs, cb=cb):
                    w0 = plsc.load_gather(w_v, [jnp.full((nl,), r, jnp.int32)])
                    w1 = plsc.load_gather(w_v, [jnp.full((nl,), r + half, jnp.int32)])
                    cols = lambda c: pl.ds(D_MODEL + (cb * ncol + c) * nl, nl)
                    return tuple(acc + (w0 * rows[0, r, cols(c)] + w1 * rows[1, r, cols(c)])
                                 for c, acc in enumerate(accs))
                sums = lax.fori_loop(0, half, add_rows, tuple(jnp.zeros((nl,), F32) for _ in range(ncol)))
                for c, v in enumerate(sums):
                    plsc.addupdate(o_v.at[pl.ds((cb * ncol + c) * nl, nl)], v)

        @pl.loop(0, per_w)
        def _(ti):
            tok = wid * per_w + ti
            base = pl.multiple_of(tok * PEER_SLOTS, PEER_SLOTS)
            pltpu.sync_copy(eid_hbm.at[pl.ds(base, PEER_SLOTS)], idx_v)
            start_group(0, rows_a, sem_a)
            pltpu.sync_copy(g_hbm.at[pl.ds(base, PEER_SLOTS)], g_v)
            pltpu.sync_copy(hn_hbm.at[tok], x_v)
            for c in range(nvec):
                o_v[pl.ds(c * nl, nl)] = jnp.zeros((nl,), F32)

            @pl.loop(0, ngroup // 2)
            def _(j):
                start_group(2 * j + 1, rows_b, sem_b)
                wait_group(2 * j, rows_a, sem_a)
                reduce_group(2 * j, rows_a)

                @pl.when(j + 1 < ngroup // 2)
                def _():
                    start_group(2 * j + 2, rows_a, sem_a)

                wait_group(2 * j + 1, rows_b, sem_b)
                reduce_group(2 * j + 1, rows_b)

            pltpu.sync_copy(o_v, out_hbm.at[tok])

    return k(eid_flat, g_flat, hn, uv_rows)


def _residual_norm_kernel(h_ref, p_ref, nf_ref, y_ref):
    y_ref[...] = _rms(h_ref[...] + p_ref[...], nf_ref[...])


def _peer(eid_t, g_t, hn, h, nf_g, uv):
    n = hn.shape[0]
    tp = PEER_TOKENS_PER_STEP
    n_sc = SC_TOKENS if n > SC_TOKENS else 0
    eid_rows = eid_t.T
    eid_tok = eid_rows.reshape(n // tp, tp * PEER_SLOTS)
    rows = PEER_CHUNK * PEER_SLOTS * PEER_PITCH
    first = n_sc // tp
    tok = pl.BlockSpec((tp, D_MODEL), lambda i: (i + first, 0))
    y_tc = pl.pallas_call(
        functools.partial(_peer_kernel, first),
        grid=((n - n_sc) // tp,),
        in_specs=[pl.BlockSpec(memory_space=pl.ANY),
                  pl.BlockSpec((PEER_SLOTS, tp), lambda i: (0, i + first)),
                  tok, tok,
                  pl.BlockSpec((1, D_MODEL), lambda i: (0, 0)),
                  pl.BlockSpec(memory_space=pl.ANY)],
        out_specs=pl.BlockSpec((tp, D_MODEL), lambda i: (i, 0)),
        out_shape=jax.ShapeDtypeStruct((n - n_sc, D_MODEL), F32),
        scratch_shapes=[pltpu.SMEM((tp * PEER_SLOTS,), jnp.int32),
                        pltpu.VMEM((2, rows, LANES), F32),
                        pltpu.SemaphoreType.DMA((3,))],
        compiler_params=_params("arbitrary"),
    )(eid_tok, g_t, hn, h, nf_g.reshape(1, D_MODEL), uv)
    if n_sc == 0:
        return y_tc
    p_sc = _peer_sc_experts(eid_rows.reshape(-1), g_t.T.reshape(-1), hn,
                            uv.reshape(-1, 2 * D_MODEL), n_sc)
    tm = 256
    blk = pl.BlockSpec((tm, D_MODEL), lambda i: (i, 0))
    y_sc = pl.pallas_call(
        _residual_norm_kernel,
        grid=(n_sc // tm,),
        in_specs=[blk, blk, pl.BlockSpec((1, D_MODEL), lambda i: (0, 0))],
        out_specs=blk,
        out_shape=jax.ShapeDtypeStruct((n_sc, D_MODEL), F32),
        compiler_params=_params("parallel"),
    )(h, p_sc, nf_g.reshape(1, D_MODEL))
    return jnp.concatenate([y_sc, y_tc], axis=0)


def _alibi_slopes():
    return (2.0 ** (-8.0 * (np.arange(ATTN_HEADS) + 1) / ATTN_HEADS)).astype(np.float32)


def _ret_consts():
    log_gamma = np.log(1.0 - 2.0 ** (-5.0 - np.arange(RET_HEADS))).astype(np.float32)
    pos = np.arange(RET_CHUNK, dtype=np.float32)
    rel = pos[:, None] - pos[None, :]
    dmat = np.where(rel[None] >= 0, np.exp(np.maximum(rel, 0.0)[None] * log_gamma[:, None, None]), 0.0)
    lanes = lambda col: np.broadcast_to(col[:, :, None], (RET_HEADS, col.shape[1], RET_HEAD_DIM))
    qdec = lanes(np.exp((pos[None, :] + 1.0) * log_gamma[:, None]))
    kdec = lanes(np.exp((RET_CHUNK - 1.0 - pos)[None, :] * log_gamma[:, None]))
    cdec = lanes(np.broadcast_to(np.exp(RET_CHUNK * log_gamma)[:, None], (RET_HEADS, 8)))
    gam = lanes(np.broadcast_to(np.exp(log_gamma)[:, None], (RET_HEADS, 8)))
    as_f32 = lambda a: jnp.asarray(np.ascontiguousarray(a, dtype=np.float32))
    return tuple(as_f32(a) for a in (dmat, qdec, kdec, cdec)), as_f32(gam)


def _channel_and_norm(attn, ret_o, gate, x, gn_g, w_out, n2_g, w_pq, keys, uv, nf_g):
    h, hn, st = _mixer_out(attn, ret_o, gate, x, gn_g, w_out, n2_g, w_pq, keys)
    eid_t, g_t = _topk(st)
    return _peer(eid_t, g_t, hn, h, nf_g, uv)


def kernel(x_prompt, x_sample, cache_k_win, cache_v_win, state_ret, norm1_g, w_in, ret_gn_g, w_out, norm2_g, w_pq, peer_sub_keys, peer_u, peer_v, norm_f_g):
    depth = w_in.shape[0]
    assert depth == 1, "single-layer stack"
    b, s, _ = x_prompt.shape
    bs = x_sample.shape[0]
    assert x_sample.shape[1] == 1 and s % (ATTN_BLOCK * max(DILATIONS)) == 0
    assert cache_k_win.shape[2] % (N_STEPS * max(DILATIONS)) == 0

    slopes = _alibi_slopes()
    slopes_tile = jnp.asarray(np.broadcast_to(slopes.reshape(ATTN_HEADS // 2, 2, 1, 1), (ATTN_HEADS // 2, 2, 8, LANES)).copy())
    ret_consts, gam_tile = _ret_consts()

    w_in_b = w_in[0].astype(MXU_DTYPE)
    w_out_b = w_out[0].astype(MXU_DTYPE)
    w_pq_b = w_pq[0].astype(MXU_DTYPE)
    keys_b = peer_sub_keys[0].reshape(2 * PEER_HEADS, PEER_N_KEYS, PEER_HALF).astype(MXU_DTYPE)
    uv = _expert_slabs(peer_u[0], peer_v[0])
    tail = (ret_gn_g[0], w_out_b, norm2_g[0], w_pq_b, keys_b, uv, norm_f_g)

    xp = x_prompt.reshape(b * s, D_MODEL)
    aq, ak, av, rq, rk, rv, rg, kt, vt = _inproj(xp, norm1_g[0], w_in_b, seq_len=s)
    seq = lambda a: a.reshape(b, s, ATTN_WIDTH)
    attn = _prompt_attention(seq(aq), seq(ak), seq(av), slopes_tile)
    ret_o, s_fin = _ret_prompt(seq(rq), seq(rk), seq(rv), ret_consts)
    y_prompt = _channel_and_norm(attn.reshape(b * s, ATTN_WIDTH), ret_o.reshape(b * s, RET_WIDTH), rg, xp, *tail)
    keep = min(N_STEPS * max(DILATIONS), s)
    window = lambda a: a.reshape(b, ATTN_HEADS, ATTN_HEAD_DIM, s).transpose(0, 3, 1, 2)[:, s - keep:]
    k_win, v_win = window(kt), window(vt)

    xs = x_sample.reshape(bs, D_MODEL)
    aq, ak, av, rq, rk, rv, rg = _inproj(xs, norm1_g[0], w_in_b)
    position_minor = lambda c: c.transpose(0, 2, 3, 1)
    attn = _sample_attention(aq, ak, av, position_minor(cache_k_win[0]), position_minor(cache_v_win[0]))
    ret_o, s_new = _ret_sample(rq, rk, rv, state_ret[0], gam_tile)
    y_sample = _channel_and_norm(attn, ret_o, rg, xs, *tail)

    return (y_prompt.reshape(b, s, D_MODEL), y_sample.reshape(bs, 1, D_MODEL),
            k_win[None], v_win[None], s_fin[None],
            ak.reshape(1, bs, 1, ATTN_HEADS, ATTN_HEAD_DIM), av.reshape(1, bs, 1, ATTN_HEADS, ATTN_HEAD_DIM),
            s_new[None])
```

```python
import functools

import numpy as np
import jax
import jax.numpy as jnp
from jax import lax
from jax.experimental import pallas as pl
from jax.experimental.pallas import tpu as pltpu
from jax.experimental.pallas import tpu_sc as plsc

F32 = jnp.float32
MXU_DTYPE = jnp.bfloat16

D_MODEL = 1024
ATTN_WIDTH = 512
RET_WIDTH = 512
ATTN_HEAD_DIM = 64
ATTN_HEADS = 8
RET_HEADS = 4
RET_HEAD_DIM = 128
DILATIONS = (1, 4, 16)
N_STEPS = 128
ATTN_BLOCK = 128
ATTN_PAD = N_STEPS * max(DILATIONS)
RET_CHUNK = 128
PEER_HEADS = 8
PEER_N_KEYS = 128
PEER_TOPK = 16
PEER_HALF = 128
PEER_SLOTS = PEER_HEADS * PEER_TOPK
NORM_EPS = 1e-6
GN_EPS = 1e-5
IN_SPLITS = 7
RET_K_SCALE = RET_HEAD_DIM ** -0.5
ATTN_SCALE = ATTN_HEAD_DIM ** -0.5
LANES = 128
VMEM_LIMIT = 48 * 1024 * 1024

PEER_TOKENS_PER_STEP = 128
PEER_CHUNK = 8
PEER_SLAB = 2 * D_MODEL // LANES
PEER_PITCH = PEER_SLAB + 1
PEER_ISSUE_UNROLL = 8
SC_TOKENS = 14592
SC_ROWS = 16


def _params(*sem):
    return pltpu.CompilerParams(dimension_semantics=sem, vmem_limit_bytes=VMEM_LIMIT)


def _rms(x, g):
    r = lax.rsqrt(jnp.mean(x * x, axis=-1, keepdims=True) + NORM_EPS)
    return x * r * g


def _inproj_kernel(x_ref, g_ref, w_ref, *refs):
    out_refs = refs[-IN_SPLITS:] if len(refs) == IN_SPLITS else refs[1:1 + IN_SPLITS]
    xn = _rms(x_ref[...], g_ref[...]).astype(MXU_DTYPE)
    for i, o_ref in enumerate(out_refs):
        y = jnp.dot(xn, w_ref[:, i * ATTN_WIDTH:(i + 1) * ATTN_WIDTH], preferred_element_type=F32)
        if i == 4:
            y = y * RET_K_SCALE
        o_ref[...] = y
    if len(refs) > IN_SPLITS:
        wkv_t_ref, kt_ref, vt_ref = refs[0], refs[-2], refs[-1]
        nt = (((1,), (1,)), ((), ()))
        kv_t = lax.dot_general(wkv_t_ref[...], xn, nt, preferred_element_type=F32)
        kt_ref[...] = kv_t[:ATTN_WIDTH]
        vt_ref[...] = kv_t[ATTN_WIDTH:]


def _inproj(x, g, w, seq_len=None):
    n = x.shape[0]
    tm = min(256, n)
    out = jax.ShapeDtypeStruct((n, ATTN_WIDTH), F32)
    in_specs = [pl.BlockSpec((tm, D_MODEL), lambda i: (i, 0)),
                pl.BlockSpec((1, D_MODEL), lambda i: (0, 0)),
                pl.BlockSpec((D_MODEL, IN_SPLITS * ATTN_WIDTH), lambda i: (0, 0))]
    out_specs = [pl.BlockSpec((tm, ATTN_WIDTH), lambda i: (i, 0))] * IN_SPLITS
    out_shape = [out] * IN_SPLITS
    args = [x, g.reshape(1, D_MODEL), w]
    if seq_len is not None:
        per_row = seq_len // tm
        in_specs.append(pl.BlockSpec((2 * ATTN_WIDTH, D_MODEL), lambda i: (0, 0)))
        args.append(w[:, ATTN_WIDTH:3 * ATTN_WIDTH].T)
        out_specs += [pl.BlockSpec((None, ATTN_WIDTH, tm), lambda i: (i // per_row, 0, i % per_row))] * 2
        out_shape += [jax.ShapeDtypeStruct((n // seq_len, ATTN_WIDTH, seq_len), F32)] * 2
    return pl.pallas_call(
        _inproj_kernel,
        grid=(n // tm,),
        in_specs=in_specs,
        out_specs=out_specs,
        out_shape=out_shape,
        compiler_params=_params("parallel"),
    )(*args)


def _merge_groups(outs, lses):
    m = functools.reduce(jnp.maximum, lses)
    es = [jnp.exp(l - m) for l in lses]
    den = functools.reduce(jnp.add, es)
    return functools.reduce(jnp.add, [(e / den) * o for e, o in zip(es, outs)])


def _prompt_attn_kernel(s_len, q_ref, k_ref, v_ref, slope_ref, o_ref, og_ref, lg_ref):
    lane = lax.broadcasted_iota(jnp.int32, (ATTN_BLOCK, LANES), 1)
    first = lane < ATTN_HEAD_DIM
    qi = lax.broadcasted_iota(jnp.int32, (ATTN_BLOCK, 2 * ATTN_BLOCK), 0)
    kj = lax.broadcasted_iota(jnp.int32, (ATTN_BLOCK, 2 * ATTN_BLOCK), 1)
    back = qi + ATTN_BLOCK - kj
    in_band = (back >= 0) & (back <= N_STEPS)
    nt = (((1,), (1,)), ((), ()))
    ntile = s_len // ATTN_BLOCK
    for g, dil in enumerate(DILATIONS):
        dist = (back * dil).astype(F32)
        bias = [jnp.where(in_band, -slope_ref[hh, 0:1, 0:1] * dist, -jnp.inf) for hh in range(2)]

        def tile(t, carry, g=g, dil=dil, bias=bias):
            r, n = t % dil, t // dil
            qrow = r + n * (dil * ATTN_BLOCK)
            krow = ATTN_PAD + qrow - dil * ATTN_BLOCK
            q = q_ref[pl.ds(qrow, ATTN_BLOCK, stride=dil), :]
            kb = k_ref[pl.ds(krow, 2 * ATTN_BLOCK, stride=dil), :].astype(MXU_DTYPE)
            vb = v_ref[pl.ds(krow, 2 * ATTN_BLOCK, stride=dil), :].astype(MXU_DTYPE)
            exists = kj + n * ATTN_BLOCK >= ATTN_BLOCK
            outs, lses = [], []
            for hh in range(2):
                qm = jnp.where(first if hh == 0 else ~first, q, 0.0).astype(MXU_DTYPE)
                sc = lax.dot_general(qm, kb, nt, preferred_element_type=F32) * ATTN_SCALE + bias[hh]
                sc = jnp.where(exists, sc, -jnp.inf)
                m = jnp.max(sc, -1, keepdims=True)
                p = jnp.exp(sc - m)
                den = jnp.sum(p, -1, keepdims=True)
                outs.append(jnp.dot((p / den).astype(MXU_DTYPE), vb, preferred_element_type=F32))
                lses.append(jnp.broadcast_to(m + jnp.log(den), (ATTN_BLOCK, LANES)))
            og_ref[g, pl.ds(qrow, ATTN_BLOCK, stride=dil), :] = jnp.where(first, outs[0], outs[1])
            lg_ref[g, pl.ds(qrow, ATTN_BLOCK, stride=dil), :] = jnp.where(first, lses[0], lses[1])
            return carry

        lax.fori_loop(0, ntile, tile, 0, unroll=16)

    def merge(i, carry):
        rows = pl.ds(pl.multiple_of(i * ATTN_BLOCK, ATTN_BLOCK), ATTN_BLOCK)
        groups = range(len(DILATIONS))
        o_ref[rows, :] = _merge_groups([og_ref[g, rows, :] for g in groups], [lg_ref[g, rows, :] for g in groups])
        return carry

    lax.fori_loop(0, ntile, merge, 0, unroll=4)


def _prompt_attention(q, k, v, slopes_tile):
    b, s, _ = q.shape
    pad = lambda a: jnp.pad(a, ((0, 0), (ATTN_PAD, 0), (0, 0)))
    q_spec = pl.BlockSpec((None, s, LANES), lambda i, h: (i, 0, h))
    kv_spec = pl.BlockSpec((None, ATTN_PAD + s, LANES), lambda i, h: (i, 0, h))
    return pl.pallas_call(
        functools.partial(_prompt_attn_kernel, s),
        grid=(b, ATTN_WIDTH // LANES),
        in_specs=[q_spec, kv_spec, kv_spec, pl.BlockSpec((None, 2, 8, LANES), lambda i, h: (h, 0, 0, 0))],
        out_specs=q_spec,
        out_shape=jax.ShapeDtypeStruct((b, s, ATTN_WIDTH), F32),
        scratch_shapes=[pltpu.VMEM((len(DILATIONS), s, LANES), F32), pltpu.VMEM((len(DILATIONS), s, LANES), F32)],
        compiler_params=_params("parallel", "parallel"),
    )(q, pad(k), pad(v), slopes_tile)


def _sample_attn_kernel(slopes, q_ref, kn_ref, vn_ref, kt_ref, vt_ref, o_ref):
    w = kt_ref.shape[-1]
    back_i = w - lax.broadcasted_iota(jnp.int32, (1, w), 1)
    back = back_i.astype(F32)
    valid = [(back_i % dil == 0) & (back_i <= N_STEPS * dil) for dil in DILATIONS]
    for hd in range(ATTN_HEADS):
        q = q_ref[hd]
        raw = jnp.sum(kt_ref[hd] * q, 0, keepdims=True) * ATTN_SCALE - slopes[hd] * back
        l0 = jnp.sum(q * kn_ref[hd], 0, keepdims=True) * ATTN_SCALE
        probs, lses = [], []
        for ok in valid:
            m = jnp.maximum(jnp.max(jnp.where(ok, raw, -jnp.inf), -1, keepdims=True), l0)
            p = jnp.where(ok, jnp.exp(raw - m), 0.0)
            p0 = jnp.exp(l0 - m)
            s = jnp.sum(p, -1, keepdims=True) + p0
            probs.append((p / s, p0 / s))
            lses.append(m + jnp.log(s))
        top = functools.reduce(jnp.maximum, lses)
        es = [jnp.exp(l - top) for l in lses]
        den = functools.reduce(jnp.add, es)
        p_all = functools.reduce(jnp.add, [(e / den) * p for e, (p, _) in zip(es, probs)])
        p0_all = functools.reduce(jnp.add, [(e / den) * p0 for e, (_, p0) in zip(es, probs)])
        o_ref[hd] = jnp.sum(vt_ref[hd] * p_all, -1, keepdims=True) + p0_all * vn_ref[hd]


def _sample_attention(q, kn, vn, cache_kt, cache_vt):
    b = q.shape[0]
    assert cache_kt.shape[-1] >= N_STEPS * max(DILATIONS)
    cols = lambda a: a.reshape(b, ATTN_HEADS, ATTN_HEAD_DIM, 1)
    col_spec = pl.BlockSpec((None, ATTN_HEADS, ATTN_HEAD_DIM, 1), lambda i: (i, 0, 0, 0))
    win_spec = pl.BlockSpec((None,) + cache_kt.shape[1:], lambda i: (i, 0, 0, 0))
    out = pl.pallas_call(
        functools.partial(_sample_attn_kernel, [float(v) for v in _alibi_slopes()]),
        grid=(b,),
        in_specs=[col_spec, col_spec, col_spec, win_spec, win_spec],
        out_specs=col_spec,
        out_shape=jax.ShapeDtypeStruct((b, ATTN_HEADS, ATTN_HEAD_DIM, 1), F32),
        compiler_params=_params("parallel"),
    )(cols(q), cols(kn), cols(vn), cache_kt, cache_vt)
    return out.reshape(b, ATTN_WIDTH)


def _ret_prompt_kernel(nchunk, q_ref, k_ref, v_ref, dmat_ref, qdec_ref, kdec_ref, cdec_ref, o_ref, s_ref):
    s_ref[...] = jnp.zeros_like(s_ref)
    nt = (((1,), (1,)), ((), ()))
    tn = (((0,), (0,)), ((), ()))

    def chunk(c, carry):
        row = pl.multiple_of(c * RET_CHUNK, RET_CHUNK)
        q = q_ref[pl.ds(row, RET_CHUNK), :]
        k = k_ref[pl.ds(row, RET_CHUNK), :]
        v = v_ref[pl.ds(row, RET_CHUNK), :].astype(MXU_DTYPE)
        qb = q.astype(MXU_DTYPE)
        state = s_ref[...]
        inner = lax.dot_general(qb, k.astype(MXU_DTYPE), nt, preferred_element_type=F32) * dmat_ref[...]
        o_in = jnp.dot(inner.astype(MXU_DTYPE), v, preferred_element_type=F32)
        o_x = jnp.dot(qb, state.astype(MXU_DTYPE), preferred_element_type=F32) * qdec_ref[...]
        o_ref[pl.ds(row, RET_CHUNK), :] = o_in + o_x
        kd = (k * kdec_ref[...]).astype(MXU_DTYPE)
        s_ref[...] = cdec_ref[0:1, :] * state + lax.dot_general(kd, v, tn, preferred_element_type=F32)
        return carry

    lax.fori_loop(0, nchunk, chunk, 0, unroll=8)


def _ret_prompt(q, k, v, consts):
    b, s, _ = q.shape
    seq = pl.BlockSpec((None, s, RET_HEAD_DIM), lambda i, h: (i, 0, h))
    per_head = lambda rows: pl.BlockSpec((None, rows, RET_HEAD_DIM), lambda i, h: (h, 0, 0))
    return pl.pallas_call(
        functools.partial(_ret_prompt_kernel, s // RET_CHUNK),
        grid=(b, RET_HEADS),
        in_specs=[seq, seq, seq, per_head(RET_CHUNK), per_head(RET_CHUNK), per_head(RET_CHUNK), per_head(8)],
        out_specs=[seq, pl.BlockSpec((None, None, RET_HEAD_DIM, RET_HEAD_DIM), lambda i, h: (i, h, 0, 0))],
        out_shape=[jax.ShapeDtypeStruct((b, s, RET_WIDTH), F32),
                   jax.ShapeDtypeStruct((b, RET_HEADS, RET_HEAD_DIM, RET_HEAD_DIM), F32)],
        compiler_params=_params("parallel", "parallel"),
    )(q, k, v, *consts)


def _ret_sample_kernel(qc_ref, kc_ref, v_ref, s_ref, gam_ref, o_ref, sn_ref):
    for h in range(RET_HEADS):
        qc = qc_ref[h]
        kc = kc_ref[h]
        v = v_ref[:, h * RET_HEAD_DIM:(h + 1) * RET_HEAD_DIM]
        gam = gam_ref[h, 0:1, :]
        state = s_ref[h]
        inner = jnp.sum(qc * kc, 0, keepdims=True)
        o_x = jnp.sum(qc * state, 0, keepdims=True)
        o_ref[:, h * RET_HEAD_DIM:(h + 1) * RET_HEAD_DIM] = inner * v + gam * o_x
        sn_ref[h] = gam * state + kc * v


def _ret_sample(q, k, v, state, gam_tile):
    b = q.shape[0]
    col = lambda a: a.reshape(b, RET_HEADS, RET_HEAD_DIM, 1)
    col_spec = pl.BlockSpec((None, RET_HEADS, RET_HEAD_DIM, 1), lambda i: (i, 0, 0, 0))
    row_spec = pl.BlockSpec((None, 1, RET_WIDTH), lambda i: (i, 0, 0))
    st_spec = pl.BlockSpec((None, RET_HEADS, RET_HEAD_DIM, RET_HEAD_DIM), lambda i: (i, 0, 0, 0))
    o, sn = pl.pallas_call(
        _ret_sample_kernel,
        grid=(b,),
        in_specs=[col_spec, col_spec, row_spec, st_spec,
                  pl.BlockSpec((RET_HEADS, 8, RET_HEAD_DIM), lambda i: (0, 0, 0))],
        out_specs=[row_spec, st_spec],
        out_shape=[jax.ShapeDtypeStruct((b, 1, RET_WIDTH), F32), jax.ShapeDtypeStruct(state.shape, F32)],
        compiler_params=_params("parallel"),
    )(col(q), col(k), v.reshape(b, 1, RET_WIDTH), state, gam_tile)
    return o.reshape(b, RET_WIDTH), sn


def _mixer_out_kernel(attn_ref, ret_ref, gate_ref, x_ref, gn_ref, wout_ref, n2_ref, wpq_ref, keys_ref,
                      h_ref, hn_ref, st_ref):
    gate = gate_ref[...]
    swish = gate * (1.0 / (1.0 + jnp.exp(-gate)))
    normed = []
    for hd in range(RET_HEADS):
        cols = slice(hd * RET_HEAD_DIM, (hd + 1) * RET_HEAD_DIM)
        r = ret_ref[:, cols]
        mu = jnp.mean(r, -1, keepdims=True)
        var = jnp.mean(jnp.square(r - mu), -1, keepdims=True)
        normed.append((r - mu) * lax.rsqrt(var + GN_EPS) * gn_ref[:, cols])
    ret_y = swish * jnp.concatenate(normed, -1)
    h = (x_ref[...]
         + jnp.dot(attn_ref[...].astype(MXU_DTYPE), wout_ref[0:ATTN_WIDTH, :], preferred_element_type=F32)
         + jnp.dot(ret_y.astype(MXU_DTYPE), wout_ref[ATTN_WIDTH:, :], preferred_element_type=F32))
    h_ref[...] = h
    hn = _rms(h, n2_ref[...])
    hn_ref[...] = hn
    hb = hn.astype(MXU_DTYPE)
    nt = (((1,), (1,)), ((), ()))
    for hd in range(PEER_HEADS):
        qh = jnp.dot(hb, wpq_ref[:, hd * 2 * PEER_HALF:(hd + 1) * 2 * PEER_HALF], preferred_element_type=F32)
        for c in range(2):
            qj = qh[:, c * PEER_HALF:(c + 1) * PEER_HALF].astype(MXU_DTYPE)
            st_ref[2 * hd + c] = lax.dot_general(keys_ref[2 * hd + c], qj, nt, preferred_element_type=F32)


def _mixer_out(attn, ret_o, gate, x, gn_g, w_out, n2_g, w_pq, keys):
    n = x.shape[0]
    tm = min(256, n)
    half = pl.BlockSpec((tm, ATTN_WIDTH), lambda i: (i, 0))
    full = pl.BlockSpec((tm, D_MODEL), lambda i: (i, 0))
    const = lambda shape: pl.BlockSpec(shape, lambda i: (0,) * len(shape))
    return pl.pallas_call(
        _mixer_out_kernel,
        grid=(n // tm,),
        in_specs=[half] * 3 + [full, const((1, RET_WIDTH)), const((D_MODEL, D_MODEL)), const((1, D_MODEL)),
                               const((D_MODEL, 2 * PEER_HEADS * PEER_HALF)),
                               const((2 * PEER_HEADS, PEER_N_KEYS, PEER_HALF))],
        out_specs=[full, full, pl.BlockSpec((2 * PEER_HEADS, PEER_N_KEYS, tm), lambda i: (0, 0, i))],
        out_shape=[jax.ShapeDtypeStruct((n, D_MODEL), F32), jax.ShapeDtypeStruct((n, D_MODEL), F32),
                   jax.ShapeDtypeStruct((2 * PEER_HEADS, PEER_N_KEYS, n), F32)],
        compiler_params=_params("parallel"),
    )(attn, ret_o, gate, x, gn_g.reshape(1, RET_WIDTH), w_out, n2_g.reshape(1, D_MODEL), w_pq, keys)


def _top_rows(s, k, payload=None):
    nrow = s.shape[0]
    ridx = lax.broadcasted_iota(jnp.int32, s.shape, 0)
    vals, picks = [], []
    for _ in range(k):
        m = jnp.max(s, 0, keepdims=True)
        am = jnp.min(jnp.where(s == m, ridx, nrow), 0, keepdims=True)
        hit = ridx == am
        vals.append(m)
        picks.append(am if payload is None else jnp.sum(jnp.where(hit, payload, 0), 0, keepdims=True))
        s = jnp.where(hit, -jnp.inf, s)
    return jnp.concatenate(vals, 0), jnp.concatenate(picks, 0)


def _product_grid(a0, a1, op):
    pieces = [op(a0[0:1], a1)]
    pieces += [op(a0[a:a + 1], a1[0:8]) for a in range(1, 8)]
    pieces.append(op(a0[8:16], a1[0:1]))
    return jnp.concatenate(pieces, 0)


def _topk_kernel(st_ref, eid_ref, g_ref):
    def head(hd, carry):
        sv0, si0 = _top_rows(st_ref[2 * hd], PEER_TOPK)
        sv1, si1 = _top_rows(st_ref[2 * hd + 1], PEER_TOPK)
        cand = _product_grid(sv0, sv1, jnp.add)
        cidx = _product_grid(si0, si1, lambda i, j: i * PEER_N_KEYS + j)
        best, eid = _top_rows(cand, PEER_TOPK, payload=cidx)
        e = jnp.exp(best - jnp.max(best, 0, keepdims=True))
        rows = pl.ds(pl.multiple_of(hd * PEER_TOPK, PEER_TOPK), PEER_TOPK)
        eid_ref[rows, :] = eid
        g_ref[rows, :] = e / jnp.sum(e, 0, keepdims=True)
        return carry

    lax.fori_loop(0, PEER_HEADS, head, 0, unroll=8)


def _topk(st):
    n = st.shape[-1]
    tn = LANES
    out_spec = pl.BlockSpec((PEER_SLOTS, tn), lambda i: (0, i))
    return pl.pallas_call(
        _topk_kernel,
        grid=(n // tn,),
        in_specs=[pl.BlockSpec((2 * PEER_HEADS, PEER_N_KEYS, tn), lambda i: (0, 0, i))],
        out_specs=[out_spec, out_spec],
        out_shape=[jax.ShapeDtypeStruct((PEER_SLOTS, n), jnp.int32), jax.ShapeDtypeStruct((PEER_SLOTS, n), F32)],
        compiler_params=_params("parallel"),
    )(st)


def _peer_kernel(first_step, eid_hbm, gt_ref, hn_ref, h_ref, nf_ref, uv_hbm, y_ref, eid_smem, buf, sems):
    step = pl.program_id(0) + first_step
    nchunk = PEER_TOKENS_PER_STEP // PEER_CHUNK
    nexp = PEER_CHUNK * PEER_SLOTS
    ids = pltpu.make_async_copy(eid_hbm.at[step], eid_smem, sems.at[2])
    ids.start()
    ids.wait()

    def start_copy(c, i, slot, k):
        e = eid_smem[c * nexp + i]
        src = uv_hbm.at[pl.ds(pl.multiple_of(e * PEER_SLAB, PEER_SLAB), PEER_SLAB), :]
        dst = buf.at[slot, pl.ds(i * PEER_PITCH, PEER_SLAB), :]
        pltpu.make_async_copy(src, dst, sems.at[slot]).start(priority=k % 2)

    def gather(c, slot):
        def group(j, carry):
            for k in range(PEER_ISSUE_UNROLL):
                start_copy(c, j * PEER_ISSUE_UNROLL + k, slot, k)
            return carry
        lax.fori_loop(0, nexp // PEER_ISSUE_UNROLL, group, 0)

    def wait(slot):
        span = pl.ds(0, nexp * PEER_SLAB)
        pltpu.make_async_copy(uv_hbm.at[span, :], buf.at[slot, span, :], sems.at[slot]).wait()

    def rows_of(slot, t, first_row):
        base = t * PEER_SLOTS * PEER_PITCH + first_row
        return jnp.concatenate(
            [jnp.concatenate([buf[slot, pl.ds(base + kg * 8 * PEER_PITCH + s, 8, stride=PEER_PITCH), :]
                              for s in range(D_MODEL // LANES)], axis=1)
             for kg in range(PEER_SLOTS // 8)], axis=0)

    tok_lane = lax.broadcasted_iota(jnp.int32, (PEER_SLOTS, PEER_TOKENS_PER_STEP), 1)
    gather(0, 0)

    def chunk(c, carry):
        slot = c % 2

        @pl.when(c + 1 < nchunk)
        def _():
            gather(c + 1, 1 - slot)

        wait(slot)
        for t in range(PEER_CHUNK):
            tok = c * PEER_CHUNK + t
            x = hn_ref[pl.ds(tok, 1), :]
            a = jax.nn.gelu(jnp.sum(rows_of(slot, t, 0) * x, -1, keepdims=True))
            gate = jnp.sum(jnp.where(tok_lane == tok, gt_ref[...], 0.0), -1, keepdims=True)
            out = jnp.sum((gate * a) * rows_of(slot, t, PEER_SLAB // 2), 0, keepdims=True)
            y_ref[pl.ds(tok, 1), :] = _rms(h_ref[pl.ds(tok, 1), :] + out, nf_ref[...])
        return carry

    lax.fori_loop(0, nchunk, chunk, 0)


def _expert_slabs(peer_u, peer_v):
    e = peer_u.shape[0]
    tiles = lambda a: a.reshape(e, D_MODEL // LANES, LANES)
    return jnp.concatenate([tiles(peer_u), tiles(peer_v)], axis=1).reshape(e * PEER_SLAB, LANES)


def _peer_sc_experts(eid_flat, g_flat, hn, uv_rows, n_sc):
    info = plsc.get_sparse_core_info()
    nw = info.num_cores * info.num_subcores
    nl = info.num_lanes
    assert SC_ROWS == nl and n_sc % nw == 0
    per_w = n_sc // nw
    ngroup = PEER_SLOTS // SC_ROWS
    nvec = D_MODEL // nl
    mesh = plsc.VectorSubcoreMesh(core_axis_name="c", subcore_axis_name="s")
    half = SC_ROWS // 2
    group_buf = pltpu.VMEM((2, half, 2 * D_MODEL), F32)

    @functools.partial(
        pl.kernel, mesh=mesh,
        out_type=jax.ShapeDtypeStruct((n_sc, D_MODEL), F32),
        scratch_types=[pltpu.VMEM((PEER_SLOTS,), jnp.int32), pltpu.VMEM((PEER_SLOTS,), F32),
                       pltpu.VMEM((D_MODEL,), F32), group_buf, group_buf,
                       pltpu.VMEM((D_MODEL,), F32), pltpu.VMEM((SC_ROWS,), F32),
                       pltpu.SemaphoreType.DMA, pltpu.SemaphoreType.DMA],
        compiler_params=pltpu.CompilerParams(needs_layout_passes=False),
    )
    def k(eid_hbm, g_hbm, hn_hbm, uv_hbm, out_hbm, idx_v, g_v, x_v, rows_a, rows_b, o_v, w_v, sem_a, sem_b):
        wid = lax.axis_index("s") * info.num_cores + lax.axis_index("c")
        lane = lax.iota(jnp.int32, nl)

        def group_copies(gi, rows, sem):
            copies = []
            for hf in range(2):
                ids = idx_v.at[pl.ds(pl.multiple_of(gi * SC_ROWS + hf * half, half), half)]
                copies.append(pltpu.make_async_copy(uv_hbm.at[ids], rows.at[hf], sem))
            return copies

        def start_group(gi, rows, sem):
            for cp in group_copies(gi, rows, sem):
                cp.start()

        def wait_group(gi, rows, sem):
            for cp in group_copies(gi, rows, sem):
                cp.wait()

        def reduce_group(gi, rows):
            nacc = 8

            def tree_sum(vs):
                while len(vs) > 1:
                    vs = [vs[i] + vs[i + 1] for i in range(0, len(vs), 2)]
                return vs[0]

            def dot_pair(r, a_vec):
                acc0 = [jnp.zeros((nl,), F32) for _ in range(nacc)]
                acc1 = [jnp.zeros((nl,), F32) for _ in range(nacc)]
                for c in range(nvec):
                    xc = x_v[pl.ds(c * nl, nl)]
                    acc0[c % nacc] = acc0[c % nacc] + rows[0, r, pl.ds(c * nl, nl)] * xc
                    acc1[c % nacc] = acc1[c % nacc] + rows[1, r, pl.ds(c * nl, nl)] * xc
                a_vec = jnp.where(lane == r, jnp.sum(tree_sum(acc0)), a_vec)
                return jnp.where(lane == r + half, jnp.sum(tree_sum(acc1)), a_vec)
            a = lax.fori_loop(0, half, dot_pair, jnp.zeros((nl,), F32))
            z = 0.7978845608028654 * (a + 0.044715 * (a * a * a))
            tanh_z = 1.0 - 2.0 / (1.0 + jnp.exp(2.0 * z))
            gates = g_v[pl.ds(pl.multiple_of(gi * SC_ROWS, SC_ROWS), SC_ROWS)]
            w_v[...] = gates * (0.5 * a * (1.0 + tanh_z))

            ncol = 16
            for cb in range(nvec // ncol):
                def add_rows(r, accs, cb=cb):
                    w0 = plsc.load_gather(w_v, [jnp.full((nl,), r, jnp.int32)])
                    w1 = plsc.load_gather(w_v, [jnp.full((nl,), r + half, jnp.int32)])
                    cols = lambda c: pl.ds(D_MODEL + (cb * ncol + c) * nl, nl)
                    return tuple(acc + (w0 * rows[0, r, cols(c)] + w1 * rows[1, r, cols(c)])
                                 for c, acc in enumerate(accs))
                sums = lax.fori_loop(0, half, add_rows, tuple(jnp.zeros((nl,), F32) for _ in range(ncol)))
                for c, v in enumerate(sums):
                    plsc.addupdate(o_v.at[pl.ds((cb * ncol + c) * nl, nl)], v)

        @pl.loop(0, per_w)
        def _(ti):
            tok = wid * per_w + ti
            base = pl.multiple_of(tok * PEER_SLOTS, PEER_SLOTS)
            pltpu.sync_copy(eid_hbm.at[pl.ds(base, PEER_SLOTS)], idx_v)
            start_group(0, rows_a, sem_a)
            pltpu.sync_copy(g_hbm.at[pl.ds(base, PEER_SLOTS)], g_v)
            pltpu.sync_copy(hn_hbm.at[tok], x_v)
            for c in range(nvec):
                o_v[pl.ds(c * nl, nl)] = jnp.zeros((nl,), F32)

            @pl.loop(0, ngroup // 2)
            def _(j):
                start_group(2 * j + 1, rows_b, sem_b)
                wait_group(2 * j, rows_a, sem_a)
                reduce_group(2 * j, rows_a)

                @pl.when(j + 1 < ngroup // 2)
                def _():
                    start_group(2 * j + 2, rows_a, sem_a)

                wait_group(2 * j + 1, rows_b, sem_b)
                reduce_group(2 * j + 1, rows_b)

            pltpu.sync_copy(o_v, out_hbm.at[tok])

    return k(eid_flat, g_flat, hn, uv_rows)


def _residual_norm_kernel(h_ref, p_ref, nf_ref, y_ref):
    y_ref[...] = _rms(h_ref[...] + p_ref[...], nf_ref[...])


def _peer(eid_t, g_t, hn, h, nf_g, uv):
    n = hn.shape[0]
    tp = PEER_TOKENS_PER_STEP
    n_sc = SC_TOKENS if n > SC_TOKENS else 0
    eid_rows = eid_t.T
    eid_tok = eid_rows.reshape(n // tp, tp * PEER_SLOTS)
    rows = PEER_CHUNK * PEER_SLOTS * PEER_PITCH
    first = n_sc // tp
    tok = pl.BlockSpec((tp, D_MODEL), lambda i: (i + first, 0))
    y_tc = pl.pallas_call(
        functools.partial(_peer_kernel, first),
        grid=((n - n_sc) // tp,),
        in_specs=[pl.BlockSpec(memory_space=pl.ANY),
                  pl.BlockSpec((PEER_SLOTS, tp), lambda i: (0, i + first)),
                  tok, tok,
                  pl.BlockSpec((1, D_MODEL), lambda i: (0, 0)),
                  pl.BlockSpec(memory_space=pl.ANY)],
        out_specs=pl.BlockSpec((tp, D_MODEL), lambda i: (i, 0)),
        out_shape=jax.ShapeDtypeStruct((n - n_sc, D_MODEL), F32),
        scratch_shapes=[pltpu.SMEM((tp * PEER_SLOTS,), jnp.int32),
                        pltpu.VMEM((2, rows, LANES), F32),
                        pltpu.SemaphoreType.DMA((3,))],
        compiler_params=_params("arbitrary"),
    )(eid_tok, g_t, hn, h, nf_g.reshape(1, D_MODEL), uv)
    if n_sc == 0:
        return y_tc
    p_sc = _peer_sc_experts(eid_rows.reshape(-1), g_t.T.reshape(-1), hn,
                            uv.reshape(-1, 2 * D_MODEL), n_sc)
    tm = 256
    blk = pl.BlockSpec((tm, D_MODEL), lambda i: (i, 0))
    y_sc = pl.pallas_call(
        _residual_norm_kernel,
        grid=(n_sc // tm,),
        in_specs=[blk, blk, pl.BlockSpec((1, D_MODEL), lambda i: (0, 0))],
        out_specs=blk,
        out_shape=jax.ShapeDtypeStruct((n_sc, D_MODEL), F32),
        compiler_params=_params("parallel"),
    )(h, p_sc, nf_g.reshape(1, D_MODEL))
    return jnp.concatenate([y_sc, y_tc], axis=0)


def _alibi_slopes():
    return (2.0 ** (-8.0 * (np.arange(ATTN_HEADS) + 1) / ATTN_HEADS)).astype(np.float32)


def _ret_consts():
    log_gamma = np.log(1.0 - 2.0 ** (-5.0 - np.arange(RET_HEADS))).astype(np.float32)
    pos = np.arange(RET_CHUNK, dtype=np.float32)
    rel = pos[:, None] - pos[None, :]
    dmat = np.where(rel[None] >= 0, np.exp(np.maximum(rel, 0.0)[None] * log_gamma[:, None, None]), 0.0)
    lanes = lambda col: np.broadcast_to(col[:, :, None], (RET_HEADS, col.shape[1], RET_HEAD_DIM))
    qdec = lanes(np.exp((pos[None, :] + 1.0) * log_gamma[:, None]))
    kdec = lanes(np.exp((RET_CHUNK - 1.0 - pos)[None, :] * log_gamma[:, None]))
    cdec = lanes(np.broadcast_to(np.exp(RET_CHUNK * log_gamma)[:, None], (RET_HEADS, 8)))
    gam = lanes(np.broadcast_to(np.exp(log_gamma)[:, None], (RET_HEADS, 8)))
    as_f32 = lambda a: jnp.asarray(np.ascontiguousarray(a, dtype=np.float32))
    return tuple(as_f32(a) for a in (dmat, qdec, kdec, cdec)), as_f32(gam)


def _channel_and_norm(attn, ret_o, gate, x, gn_g, w_out, n2_g, w_pq, keys, uv, nf_g):
    h, hn, st = _mixer_out(attn, ret_o, gate, x, gn_g, w_out, n2_g, w_pq, keys)
    eid_t, g_t = _topk(st)
    return _peer(eid_t, g_t, hn, h, nf_g, uv)


def kernel(x_prompt, x_sample, cache_k_win, cache_v_win, state_ret, norm1_g, w_in, ret_gn_g, w_out, norm2_g, w_pq, peer_sub_keys, peer_u, peer_v, norm_f_g):
    depth = w_in.shape[0]
    assert depth == 1, "single-layer stack"
    b, s, _ = x_prompt.shape
    bs = x_sample.shape[0]
    assert x_sample.shape[1] == 1 and s % (ATTN_BLOCK * max(DILATIONS)) == 0
    assert cache_k_win.shape[2] % (N_STEPS * max(DILATIONS)) == 0

    slopes = _alibi_slopes()
    slopes_tile = jnp.asarray(np.broadcast_to(slopes.reshape(ATTN_HEADS // 2, 2, 1, 1), (ATTN_HEADS // 2, 2, 8, LANES)).copy())
    ret_consts, gam_tile = _ret_consts()

    w_in_b = w_in[0].astype(MXU_DTYPE)
    w_out_b = w_out[0].astype(MXU_DTYPE)
    w_pq_b = w_pq[0].astype(MXU_DTYPE)
    keys_b = peer_sub_keys[0].reshape(2 * PEER_HEADS, PEER_N_KEYS, PEER_HALF).astype(MXU_DTYPE)
    uv = _expert_slabs(peer_u[0], peer_v[0])
    tail = (ret_gn_g[0], w_out_b, norm2_g[0], w_pq_b, keys_b, uv, norm_f_g)

    xp = x_prompt.reshape(b * s, D_MODEL)
    aq, ak, av, rq, rk, rv, rg, kt, vt = _inproj(xp, norm1_g[0], w_in_b, seq_len=s)
    seq = lambda a: a.reshape(b, s, ATTN_WIDTH)
    attn = _prompt_attention(seq(aq), seq(ak), seq(av), slopes_tile)
    ret_o, s_fin = _ret_prompt(seq(rq), seq(rk), seq(rv), ret_consts)
    y_prompt = _channel_and_norm(attn.reshape(b * s, ATTN_WIDTH), ret_o.reshape(b * s, RET_WIDTH), rg, xp, *tail)
    keep = min(N_STEPS * max(DILATIONS), s)
    window = lambda a: a.reshape(b, ATTN_HEADS, ATTN_HEAD_DIM, s).transpose(0, 3, 1, 2)[:, s - keep:]
    k_win, v_win = window(kt), window(vt)

    xs = x_sample.reshape(bs, D_MODEL)
    aq, ak, av, rq, rk, rv, rg = _inproj(xs, norm1_g[0], w_in_b)
    position_minor = lambda c: c.transpose(0, 2, 3, 1)
    attn = _sample_attention(aq, ak, av, position_minor(cache_k_win[0]), position_minor(cache_v_win[0]))
    ret_o, s_new = _ret_sample(rq, rk, rv, state_ret[0], gam_tile)
    y_sample = _channel_and_norm(attn, ret_o, rg, xs, *tail)

    return (y_prompt.reshape(b, s, D_MODEL), y_sample.reshape(bs, 1, D_MODEL),
            k_win[None], v_win[None], s_fin[None],
            ak.reshape(1, bs, 1, ATTN_HEADS, ATTN_HEAD_DIM), av.reshape(1, bs, 1, ATTN_HEADS, ATTN_HEAD_DIM),
            s_new[None])
```
